```python
import jax, jax.numpy as jnp
from jax import lax
import numpy as np

D_MODEL = 1024
BATCH = 2
SEQ = 16384
DEPTH = 1
DEC_BATCH = 8
DEC_SEQ = 4096
PAST_LEN = 128

GRID_W = 64
MIX_WIDTH = D_MODEL
A_WIDTH = MIX_WIDTH // 2
B_WIDTH = MIX_WIDTH - A_WIDTH
HEAD_DIM = 64
A_GROUPS = A_WIDTH // HEAD_DIM
B_HEADS = B_WIDTH // HEAD_DIM
IN_WIDTH = 2 * A_WIDTH + 3 * B_WIDTH
CHUNK = 128
WIN_ROWS_MAX = 8
WIN_COLS = 16
COL_BLOCK = 16
COL_BAND = 32
N_COL_BLOCKS = GRID_W // COL_BLOCK
D_FF = 2816
CONV_W = 3
EPS = 1e-6
NEG_INF = -1e30

kernel_name = 'hybrid_gmlp_natten_encoder'


def _rmsnorm(x, g):
    xf = x.astype(jnp.float32)
    y = xf * lax.rsqrt(jnp.mean(xf * xf, axis=-1, keepdims=True) + EPS)
    return (y * g.astype(jnp.float32)).astype(x.dtype)


def _col_tables():
    j = np.arange(N_COL_BLOCKS)
    band_start = np.clip(j * COL_BLOCK - WIN_COLS // 2, 0, GRID_W - COL_BAND)
    key_col = band_start[:, None] + np.arange(COL_BAND)[None, :]
    q_col = j[:, None] * COL_BLOCK + np.arange(COL_BLOCK)[None, :]
    win_start = np.clip(q_col - WIN_COLS // 2, 0, GRID_W - WIN_COLS)
    kc = key_col[:, None, :]
    valid = (kc >= win_start[..., None]) & (kc < win_start[..., None] + WIN_COLS)
    dc = np.clip(kc - q_col[..., None], -(WIN_COLS - 1), WIN_COLS - 1) + WIN_COLS - 1
    return key_col, valid, dc


def _neighbourhood_attention(q, k, v, rpb):
    B, T, H, Dh = q.shape
    rows = T // GRID_W
    kh = min(WIN_ROWS_MAX, rows)
    key_col, valid_np, dc = _col_tables()
    valid = jnp.asarray(valid_np)
    qg = q.reshape(B, rows, N_COL_BLOCKS, COL_BLOCK, H, Dh) * (Dh ** -0.5)
    kg = k.reshape(B, rows, GRID_W, H, Dh)
    vg = v.reshape(B, rows, GRID_W, H, Dh)
    bias_cols = rpb[:, :, dc]

    def row_step(r):
        rs = jnp.clip(r - kh // 2, 0, rows - kh)
        k_rows = lax.dynamic_slice_in_dim(kg, rs, kh, axis=1)
        v_rows = lax.dynamic_slice_in_dim(vg, rs, kh, axis=1)
        k_band = k_rows[:, :, key_col]
        v_band = v_rows[:, :, key_col]
        q_r = lax.dynamic_index_in_dim(qg, r, axis=1, keepdims=False)
        s = jnp.einsum('bjqhd,bkjwhd->bhjqkw', q_r, k_band,
                       preferred_element_type=jnp.float32)
        dr = rs + jnp.arange(kh) - r + WIN_ROWS_MAX - 1
        bias = jnp.transpose(bias_cols[:, dr], (0, 2, 3, 1, 4))
        s = s + bias[None].astype(jnp.float32)
        s = jnp.where(valid[None, None, :, :, None, :], s, NEG_INF)
        p = jax.nn.softmax(s.reshape(B, H, N_COL_BLOCKS, COL_BLOCK, kh * COL_BAND), axis=-1)
        p = p.reshape(s.shape).astype(v.dtype)
        o = jnp.einsum('bhjqkw,bkjwhd->bjqhd', p, v_band)
        return o.reshape(B, GRID_W, H * Dh)

    out = lax.map(row_step, jnp.arange(rows))
    return jnp.transpose(out, (1, 0, 2, 3)).reshape(B, T, H * Dh)


def _spatial_gating(u, v, w_s, b_s):
    B, T, G, Dg = v.shape
    vc = v.reshape(B, T // CHUNK, CHUNK, G, Dg)
    s = jnp.einsum('gts,bcsgd->bctgd', w_s, vc) + b_s.T[None, None, :, :, None]
    return u * s.reshape(B, T, G, Dg)


def _token_mixing(h, w_in, gate_norm_g, w_spatial, b_spatial, rpb, w_out):
    B, T, _ = h.shape
    z = h @ w_in
    za, zq, zk, zv = jnp.split(z, [2 * A_WIDTH, 2 * A_WIDTH + B_WIDTH, 2 * A_WIDTH + 2 * B_WIDTH], axis=-1)
    za = jax.nn.gelu(za, approximate=False)
    u, va = jnp.split(za, 2, axis=-1)
    va = _rmsnorm(va, gate_norm_g)
    ya = _spatial_gating(u.reshape(B, T, A_GROUPS, HEAD_DIM), va.reshape(B, T, A_GROUPS, HEAD_DIM),
                         w_spatial, b_spatial).reshape(B, T, A_WIDTH)
    yb = _neighbourhood_attention(zq.reshape(B, T, B_HEADS, HEAD_DIM), zk.reshape(B, T, B_HEADS, HEAD_DIM),
                                  zv.reshape(B, T, B_HEADS, HEAD_DIM), rpb)
    return jnp.concatenate([ya, yb], axis=-1) @ w_out


def _conv_glu(h, w_up, conv_w, conv_b, w_down):
    a, b = jnp.split(h @ w_up, 2, axis=-1)
    a = lax.conv_general_dilated(a, conv_w[:, None, :].astype(a.dtype), window_strides=(1,),
                                 padding=((CONV_W // 2, CONV_W // 2),),
                                 dimension_numbers=('NWC', 'WIO', 'NWC'),
                                 feature_group_count=D_FF) + conv_b
    return (jax.nn.gelu(a, approximate=False) * b) @ w_down


def _trunk(x, norm1_g, w_in, gate_norm_g, w_spatial, b_spatial, rpb, w_out,
           norm2_g, w_up, conv_w, conv_b, w_down, final_norm_g):
    for l in range(DEPTH):
        x = x + _token_mixing(_rmsnorm(x, norm1_g[l]), w_in[l], gate_norm_g[l], w_spatial[l],
                              b_spatial[l], rpb[l], w_out[l])
        x = x + _conv_glu(_rmsnorm(x, norm2_g[l]), w_up[l], conv_w[l], conv_b[l], w_down[l])
    return _rmsnorm(x, final_norm_g)


def setup_inputs(seed: int = 0) -> dict:
    key = jax.random.key(seed)
    ks = jax.random.split(key, 15)
    f32 = jnp.float32
    n = lambda k, s, sc: jax.random.normal(k, s, f32) * sc
    return {
        'x_prompt': n(ks[0], (BATCH, SEQ, D_MODEL), 1.0),
        'x_sample': n(ks[1], (DEC_BATCH, DEC_SEQ, D_MODEL), 1.0),
        'norm1_g': 1.0 + n(ks[2], (DEPTH, D_MODEL), 0.02),
        'w_in': n(ks[3], (DEPTH, D_MODEL, IN_WIDTH), D_MODEL ** -0.5),
        'gate_norm_g': 1.0 + n(ks[4], (DEPTH, A_WIDTH), 0.02),
        'w_spatial': n(ks[5], (DEPTH, A_GROUPS, CHUNK, CHUNK), CHUNK ** -0.5),
        'b_spatial': 1.0 + n(ks[6], (DEPTH, A_GROUPS, CHUNK), 0.01),
        'rpb': n(ks[7], (DEPTH, B_HEADS, 2 * WIN_ROWS_MAX - 1, 2 * WIN_COLS - 1), 0.02),
        'w_out': n(ks[8], (DEPTH, MIX_WIDTH, D_MODEL), MIX_WIDTH ** -0.5),
        'norm2_g': 1.0 + n(ks[9], (DEPTH, D_MODEL), 0.02),
        'w_up': n(ks[10], (DEPTH, D_MODEL, 2 * D_FF), D_MODEL ** -0.5),
        'conv_w': n(ks[11], (DEPTH, CONV_W, D_FF), CONV_W ** -0.5),
        'conv_b': n(ks[12], (DEPTH, D_FF), 0.01),
        'w_down': n(ks[13], (DEPTH, D_FF, D_MODEL), D_FF ** -0.5),
        'final_norm_g': 1.0 + n(ks[14], (D_MODEL,), 0.02),
    }


def reference(x_prompt, x_sample, norm1_g, w_in, gate_norm_g, w_spatial, b_spatial, rpb, w_out,
              norm2_g, w_up, conv_w, conv_b, w_down, final_norm_g):
    y_prompt = _trunk(x_prompt, norm1_g, w_in, gate_norm_g, w_spatial, b_spatial, rpb, w_out,
                      norm2_g, w_up, conv_w, conv_b, w_down, final_norm_g)
    y_sample = _trunk(x_sample, norm1_g, w_in, gate_norm_g, w_spatial, b_spatial, rpb, w_out,
                      norm2_g, w_up, conv_w, conv_b, w_down, final_norm_g)
    return (y_prompt, y_sample)
```

```python
import functools
import math

import numpy as np
import jax
import jax.numpy as jnp
from jax import lax
from jax.experimental import pallas as pl
from jax.experimental.pallas import tpu as pltpu

D_MODEL = 1024
GRID_W = 64
A_WIDTH = 512
B_WIDTH = 512
HEAD_DIM = 64
A_GROUPS = 8
B_HEADS = 8
IN_WIDTH = 2 * A_WIDTH + 3 * B_WIDTH
CHUNK = 128
WIN_ROWS = 8
WIN_COLS = 16
D_FF = 2816
EPS = 1e-6
NEG_INF = -1e30
SQRT_HALF = math.sqrt(0.5)

HEADS_PER_GROUP = 4
GROUP_W = HEADS_PER_GROUP * HEAD_DIM
N_HEAD_GROUPS = B_HEADS // HEADS_PER_GROUP
WIN_KEYS = WIN_ROWS * GRID_W

MIX_TM = 512
ATT_ROWS = 16
ATT_HALO_ROWS = 8
FFN_TM = 256
HALO_F32 = 8
HALO_BF16 = 16
VMEM_LIMIT = 56 * 1024 * 1024


def _gelu(x):
    return 0.5 * x * (1.0 + lax.erf(x * SQRT_HALF))


def _rms(x, g):
    return x * lax.rsqrt(jnp.mean(x * x, axis=-1, keepdims=True) + EPS) * g


def _const_spec(shape):
    nd = len(shape)
    return pl.BlockSpec(shape, lambda *_: (0,) * nd, pipeline_mode=pl.Buffered(1))


def _mix_in_kernel(x_ref, g1_ref, win_ref, gg_ref, ws_ref, bs_ref,
                   ya_ref, q_ref, k_ref, v_ref):
    tm = x_ref.shape[0]
    h = _rms(x_ref[...], g1_ref[...]).astype(jnp.bfloat16)
    z = jnp.dot(h, win_ref[...], preferred_element_type=jnp.float32)
    u = _gelu(z[:, :A_WIDTH])
    va = _rms(_gelu(z[:, A_WIDTH:2 * A_WIDTH]), gg_ref[...]).astype(jnp.bfloat16)
    q_ref[...] = (z[:, 2 * A_WIDTH:2 * A_WIDTH + B_WIDTH] * (HEAD_DIM ** -0.5)).astype(jnp.bfloat16)
    k_ref[...] = z[:, 2 * A_WIDTH + B_WIDTH:2 * A_WIDTH + 2 * B_WIDTH].astype(jnp.bfloat16)
    v_ref[...] = z[:, 2 * A_WIDTH + 2 * B_WIDTH:].astype(jnp.bfloat16)

    lane = lax.broadcasted_iota(jnp.int32, (CHUNK, 128), 1)
    first_half = lane < HEAD_DIM
    for c in range(tm // CHUNK):
        rows = slice(c * CHUNK, (c + 1) * CHUNK)
        for j in range(A_WIDTH // 128):
            cols = slice(j * 128, (j + 1) * 128)
            va_t = va[rows, cols]
            s0 = jnp.dot(ws_ref[2 * j], va_t, preferred_element_type=jnp.float32)
            s1 = jnp.dot(ws_ref[2 * j + 1], va_t, preferred_element_type=jnp.float32)
            s = jnp.where(first_half, s0, s1) + bs_ref[:, cols]
            ya_ref[rows, cols] = (u[rows, cols] * s).astype(jnp.bfloat16)


def _mix_in(x2d, p):
    n = x2d.shape[0]
    tm = MIX_TM
    tok = lambda w: pl.BlockSpec((tm, w), lambda i: (i, 0))
    out = jax.ShapeDtypeStruct((n, A_WIDTH), jnp.bfloat16)
    return pl.pallas_call(
        _mix_in_kernel,
        grid=(n // tm,),
        in_specs=[tok(D_MODEL), _const_spec((1, D_MODEL)), _const_spec((D_MODEL, IN_WIDTH)),
                  _const_spec((1, A_WIDTH)), _const_spec((A_GROUPS, CHUNK, CHUNK)),
                  _const_spec((CHUNK, A_WIDTH))],
        out_specs=[tok(A_WIDTH)] * 4,
        out_shape=[out] * 4,
        compiler_params=pltpu.CompilerParams(dimension_semantics=("arbitrary",),
                                             vmem_limit_bytes=VMEM_LIMIT),
        name="mix_in",
    )(x2d, p["g1"], p["w_in"], p["gate_g"], p["w_s"], p["b_s"])


def _natten_kernel(q_ref, kp_ref, kc_ref, kn_ref, vp_ref, vc_ref, vn_ref, bias_ref,
                   o_ref, kbuf, vbuf, *, n_rows):
    i = pl.program_id(1)
    halo = ATT_HALO_ROWS * GRID_W
    cur = ATT_ROWS * GRID_W
    kbuf[0:halo, :] = kp_ref[...]
    kbuf[halo:halo + cur, :] = kc_ref[...]
    kbuf[halo + cur:, :] = kn_ref[...]
    vbuf[0:halo, :] = vp_ref[...]
    vbuf[halo:halo + cur, :] = vc_ref[...]
    vbuf[halo + cur:, :] = vn_ref[...]

    row_blk = lax.broadcasted_iota(jnp.int32, (GROUP_W, GROUP_W), 0) // HEAD_DIM
    lane_blk = lax.broadcasted_iota(jnp.int32, (GROUP_W, GROUP_W), 1) // HEAD_DIM
    diag = row_blk == lane_blk
    out_lane_blk = lax.broadcasted_iota(jnp.int32, (GRID_W, GROUP_W), 1) // HEAD_DIM

    def row_body(rl, carry):
        r = i * ATT_ROWS + rl
        rs = jnp.clip(r - WIN_ROWS // 2, 0, n_rows - WIN_ROWS)
        cls = r - rs
        koff = pl.multiple_of((rs - i * ATT_ROWS + ATT_HALO_ROWS) * GRID_W, GRID_W)
        qoff = pl.multiple_of(rl * GRID_W, GRID_W)
        for g in range(N_HEAD_GROUPS):
            lanes = slice(g * GROUP_W, (g + 1) * GROUP_W)
            q_g = q_ref[pl.ds(qoff, GRID_W), lanes]
            q_bd = jnp.where(diag, jnp.concatenate([q_g] * HEADS_PER_GROUP, axis=0),
                             jnp.zeros((), jnp.bfloat16))
            k_w = kbuf[pl.ds(koff, WIN_KEYS), lanes]
            s = lax.dot_general(q_bd, k_w, (((1,), (1,)), ((), ())),
                                preferred_element_type=jnp.float32)
            s = s + bias_ref[cls, g]
            m = jnp.max(s, axis=-1, keepdims=True)
            e = jnp.exp(s - m)
            l = jnp.sum(e, axis=-1, keepdims=True)
            v_w = vbuf[pl.ds(koff, WIN_KEYS), lanes]
            o = jnp.dot(e.astype(jnp.bfloat16), v_w, preferred_element_type=jnp.float32)
            o = o * (1.0 / l)
            acc = o[0:GRID_W]
            for hh in range(1, HEADS_PER_GROUP):
                acc = jnp.where(out_lane_blk == hh, o[hh * GRID_W:(hh + 1) * GRID_W], acc)
            o_ref[pl.ds(qoff, GRID_W), lanes] = acc.astype(jnp.bfloat16)
        return carry

    lax.fori_loop(0, ATT_ROWS, row_body, 0)


def _natten(q, k, v, bias):
    b, t, _ = q.shape
    n_rows = t // GRID_W
    cur = ATT_ROWS * GRID_W
    halo = ATT_HALO_ROWS * GRID_W
    per = ATT_ROWS // ATT_HALO_ROWS
    n_halo_blocks = t // halo
    cur_spec = pl.BlockSpec((None, cur, B_WIDTH), lambda bi, i: (bi, i, 0))
    prev_spec = pl.BlockSpec((None, halo, B_WIDTH),
                             lambda bi, i: (bi, jnp.maximum(i * per - 1, 0), 0))
    next_spec = pl.BlockSpec((None, halo, B_WIDTH),
                             lambda bi, i: (bi, jnp.minimum((i + 1) * per, n_halo_blocks - 1), 0))
    return pl.pallas_call(
        functools.partial(_natten_kernel, n_rows=n_rows),
        grid=(b, n_rows // ATT_ROWS),
        in_specs=[cur_spec, prev_spec, cur_spec, next_spec, prev_spec, cur_spec, next_spec,
                  _const_spec(bias.shape)],
        out_specs=cur_spec,
        out_shape=jax.ShapeDtypeStruct((b, t, B_WIDTH), jnp.bfloat16),
        scratch_shapes=[pltpu.VMEM((cur + 2 * halo, B_WIDTH), jnp.bfloat16),
                        pltpu.VMEM((cur + 2 * halo, B_WIDTH), jnp.bfloat16)],
        compiler_params=pltpu.CompilerParams(dimension_semantics=("arbitrary", "arbitrary"),
                                             vmem_limit_bytes=VMEM_LIMIT),
        name="natten",
    )(q, k, k, k, v, v, v, bias)


def _ffn_out_kernel(xp_ref, x_ref, xn_ref, yap_ref, ya_ref, yan_ref, ybp_ref, yb_ref, ybn_ref,
                    woa_ref, wob_ref, g2_ref, wa_ref, wb_ref, cw_ref, cb_ref, wd_ref, gf_ref,
                    o_ref, *, tiles_per_seq):
    tm = x_ref.shape[0]
    j = pl.program_id(0) % tiles_per_seq
    has_prev = j > 0
    has_next = j < tiles_per_seq - 1
    lo, hi = HALO_F32, HALO_F32 + tm

    def ext(prev_ref, cur_ref, next_ref):
        prev = prev_ref[...].astype(jnp.float32)
        nxt = next_ref[...].astype(jnp.float32)
        prev = jnp.where(has_prev, prev[prev.shape[0] - HALO_F32:], 0.0)
        nxt = jnp.where(has_next, nxt[:HALO_F32], 0.0)
        return jnp.concatenate([prev, cur_ref[...].astype(jnp.float32), nxt], axis=0)

    x_e = ext(xp_ref, x_ref, xn_ref)
    ya_e = ext(yap_ref, ya_ref, yan_ref).astype(jnp.bfloat16)
    yb_e = ext(ybp_ref, yb_ref, ybn_ref).astype(jnp.bfloat16)
    x1 = (x_e + jnp.dot(ya_e, woa_ref[...], preferred_element_type=jnp.float32)
          + jnp.dot(yb_e, wob_ref[...], preferred_element_type=jnp.float32))
    h2 = _rms(x1, g2_ref[...])
    a_e = jnp.dot(h2.astype(jnp.bfloat16), wa_ref[...], preferred_element_type=jnp.float32)
    b = jnp.dot(h2[lo:hi].astype(jnp.bfloat16), wb_ref[...], preferred_element_type=jnp.float32)
    n_ext = tm + 2 * HALO_F32
    a_prev = pltpu.roll(a_e, 1, axis=0)[lo:hi]
    a_next = pltpu.roll(a_e, n_ext - 1, axis=0)[lo:hi]
    a_conv = (cw_ref[0:1, :] * a_prev + cw_ref[1:2, :] * a_e[lo:hi] + cw_ref[2:3, :] * a_next
              + cb_ref[...])
    gated = (_gelu(a_conv) * b).astype(jnp.bfloat16)
    x2 = x1[lo:hi] + jnp.dot(gated, wd_ref[...], preferred_element_type=jnp.float32)
    o_ref[...] = _rms(x2, gf_ref[...])


def _ffn_out(x2d, ya, yb, p, seq_len):
    n = x2d.shape[0]
    tm = FFN_TM
    tiles_per_seq = seq_len // tm
    f32_per = tm // HALO_F32
    bf_per = tm // HALO_BF16

    def halo_specs(width, rows, per):
        last = n // rows - 1
        prev = pl.BlockSpec((rows, width), lambda i: (jnp.maximum(i * per - 1, 0), 0))
        cur = pl.BlockSpec((tm, width), lambda i: (i, 0))
        nxt = pl.BlockSpec((rows, width), lambda i: (jnp.minimum((i + 1) * per, last), 0))
        return [prev, cur, nxt]

    in_specs = (halo_specs(D_MODEL, HALO_F32, f32_per) + halo_specs(A_WIDTH, HALO_BF16, bf_per)
                + halo_specs(B_WIDTH, HALO_BF16, bf_per)
                + [_const_spec((A_WIDTH, D_MODEL)), _const_spec((B_WIDTH, D_MODEL)),
                   _const_spec((1, D_MODEL)), _const_spec((D_MODEL, D_FF)),
                   _const_spec((D_MODEL, D_FF)), _const_spec((3, D_FF)), _const_spec((1, D_FF)),
                   _const_spec((D_FF, D_MODEL)), _const_spec((1, D_MODEL))])
    return pl.pallas_call(
        functools.partial(_ffn_out_kernel, tiles_per_seq=tiles_per_seq),
        grid=(n // tm,),
        in_specs=in_specs,
        out_specs=pl.BlockSpec((tm, D_MODEL), lambda i: (i, 0)),
        out_shape=jax.ShapeDtypeStruct((n, D_MODEL), jnp.float32),
        compiler_params=pltpu.CompilerParams(dimension_semantics=("arbitrary",),
                                             vmem_limit_bytes=VMEM_LIMIT),
        name="ffn_out",
    )(x2d, x2d, x2d, ya, ya, ya, yb, yb, yb,
      p["w_out_a"], p["w_out_b"], p["g2"], p["w_up_a"], p["w_up_b"], p["conv_w"], p["conv_b"],
      p["w_down"], p["gf"])


def _attention_bias(rpb):
    qc = np.arange(GRID_W)[:, None]
    kc = np.arange(GRID_W)[None, :]
    win_start = np.clip(qc - WIN_COLS // 2, 0, GRID_W - WIN_COLS)
    valid = (kc >= win_start) & (kc < win_start + WIN_COLS)
    dc = np.clip(kc - qc, -(WIN_COLS - 1), WIN_COLS - 1) + WIN_COLS - 1
    cls = np.arange(WIN_ROWS)[:, None]
    krow = np.arange(WIN_ROWS)[None, :]
    dr = krow - cls + WIN_ROWS - 1
    t = rpb[:, :, dc]
    t = jnp.where(jnp.asarray(valid)[None, None], t, NEG_INF)
    t = t[:, dr]
    t = t.reshape(N_HEAD_GROUPS, HEADS_PER_GROUP, WIN_ROWS, WIN_ROWS, GRID_W, GRID_W)
    t = jnp.transpose(t, (2, 0, 1, 4, 3, 5))
    return t.reshape(WIN_ROWS, N_HEAD_GROUPS, HEADS_PER_GROUP * GRID_W, WIN_KEYS).astype(jnp.float32)


def _prepare(norm1_g, w_in, gate_norm_g, w_spatial, b_spatial, rpb, w_out, norm2_g, w_up,
             conv_w, conv_b, w_down, final_norm_g):
    bf = jnp.bfloat16
    return {
        "g1": norm1_g[0][None, :],
        "w_in": w_in[0].astype(bf),
        "gate_g": gate_norm_g[0][None, :],
        "w_s": w_spatial[0].astype(bf),
        "b_s": jnp.repeat(b_spatial[0].T, HEAD_DIM, axis=1),
        "bias": _attention_bias(rpb[0]),
        "w_out_a": w_out[0, :A_WIDTH].astype(bf),
        "w_out_b": w_out[0, A_WIDTH:].astype(bf),
        "g2": norm2_g[0][None, :],
        "w_up_a": w_up[0, :, :D_FF].astype(bf),
        "w_up_b": w_up[0, :, D_FF:].astype(bf),
        "conv_w": conv_w[0],
        "conv_b": conv_b[0][None, :],
        "w_down": w_down[0].astype(bf),
        "gf": final_norm_g[None, :],
    }


def _trunk(x, p):
    b, t, d = x.shape
    x2d = x.reshape(b * t, d)
    ya, q, k, v = _mix_in(x2d, p)
    shp = (b, t, B_WIDTH)
    yb = _natten(q.reshape(shp), k.reshape(shp), v.reshape(shp), p["bias"])
    y = _ffn_out(x2d, ya, yb.reshape(b * t, B_WIDTH), p, t)
    return y.reshape(b, t, d)


def kernel(x_prompt, x_sample, norm1_g, w_in, gate_norm_g, w_spatial, b_spatial, rpb, w_out,
           norm2_g, w_up, conv_w, conv_b, w_down, final_norm_g):
    p = _prepare(norm1_g, w_in, gate_norm_g, w_spatial, b_spatial, rpb, w_out, norm2_g, w_up,
                 conv_w, conv_b, w_down, final_norm_g)
    return (_trunk(x_prompt, p), _trunk(x_sample, p))
```

```python
import functools
import math

import numpy as np
import jax
import jax.numpy as jnp
from jax import lax
from jax.experimental import pallas as pl
from jax.experimental.pallas import tpu as pltpu

D_MODEL = 1024
GRID_W = 64
A_WIDTH = 512
B_WIDTH = 512
HEAD_DIM = 64
A_GROUPS = 8
B_HEADS = 8
IN_WIDTH = 2 * A_WIDTH + 3 * B_WIDTH
CHUNK = 128
WIN_ROWS = 8
WIN_COLS = 16
D_FF = 2816
EPS = 1e-6
NEG_INF = -1e30
SQRT_HALF = math.sqrt(0.5)
LOG2E = math.log2(math.e)

HEADS_PER_GROUP = 4
GROUP_W = HEADS_PER_GROUP * HEAD_DIM
N_HEAD_GROUPS = B_HEADS // HEADS_PER_GROUP
WIN_KEYS = WIN_ROWS * GRID_W

MIX_TM = 1024
ATT_ROWS = 32
ATT_HALO_ROWS = 8
FFN_TM = 512
HALO_F32 = 8
HALO_BF16 = 16
VMEM_LIMIT = 56 * 1024 * 1024


def _gelu(x):
    return 0.5 * x * (1.0 + lax.erf(x * SQRT_HALF))


def _rms(x, g):
    return x * lax.rsqrt(jnp.mean(x * x, axis=-1, keepdims=True) + EPS) * g


def _const_spec(shape, block_index=None):
    idx = (0,) * len(shape) if block_index is None else block_index
    return pl.BlockSpec(shape, lambda *_: idx, pipeline_mode=pl.Buffered(1))


def _mix_in_kernel(x_ref, g1_ref, win_ref, gg_ref, ws_ref, bs_ref,
                   ya_ref, q_ref, k_ref, v_ref):
    tm = x_ref.shape[0]
    h = _rms(x_ref[...], g1_ref[...]).astype(jnp.bfloat16)
    z = jnp.dot(h, win_ref[...], preferred_element_type=jnp.float32)
    u = _gelu(z[:, :A_WIDTH])
    va = _rms(_gelu(z[:, A_WIDTH:2 * A_WIDTH]), gg_ref[...]).astype(jnp.bfloat16)
    q_ref[...] = (z[:, 2 * A_WIDTH:2 * A_WIDTH + B_WIDTH] * (HEAD_DIM ** -0.5 * LOG2E)).astype(jnp.bfloat16)
    k_ref[...] = z[:, 2 * A_WIDTH + B_WIDTH:2 * A_WIDTH + 2 * B_WIDTH].astype(jnp.bfloat16)
    v_ref[...] = z[:, 2 * A_WIDTH + 2 * B_WIDTH:].astype(jnp.bfloat16)

    first_half = lax.broadcasted_iota(jnp.int32, (CHUNK, 128), 1) < HEAD_DIM
    for c in range(tm // CHUNK):
        rows = slice(c * CHUNK, (c + 1) * CHUNK)
        for j in range(A_WIDTH // 128):
            cols = slice(j * 128, (j + 1) * 128)
            va_t = va[rows, cols]
            s0 = jnp.dot(ws_ref[2 * j], va_t, preferred_element_type=jnp.float32)
            s1 = jnp.dot(ws_ref[2 * j + 1], va_t, preferred_element_type=jnp.float32)
            s = jnp.where(first_half, s0, s1) + bs_ref[:, cols]
            ya_ref[rows, cols] = (u[rows, cols] * s).astype(jnp.bfloat16)


def _mix_in(x2d, p):
    n = x2d.shape[0]
    tm = MIX_TM
    tok = lambda w: pl.BlockSpec((tm, w), lambda i: (i, 0))
    out = jax.ShapeDtypeStruct((n, A_WIDTH), jnp.bfloat16)
    return pl.pallas_call(
        _mix_in_kernel,
        grid=(n // tm,),
        in_specs=[tok(D_MODEL), _const_spec((1, D_MODEL)), _const_spec((D_MODEL, IN_WIDTH)),
                  _const_spec((1, A_WIDTH)), _const_spec((A_GROUPS, CHUNK, CHUNK)),
                  _const_spec((CHUNK, A_WIDTH))],
        out_specs=[tok(A_WIDTH)] * 4,
        out_shape=[out] * 4,
        compiler_params=pltpu.CompilerParams(dimension_semantics=("arbitrary",),
                                             vmem_limit_bytes=VMEM_LIMIT),
        name="mix_in",
    )(x2d, p["g1"], p["w_in"], p["gate_g"], p["w_s"], p["b_s"])


def _natten_kernel(q_ref, kp_ref, kc_ref, kn_ref, vp_ref, vc_ref, vn_ref, bias_ref,
                   o_ref, kbuf, vbuf, *, n_rows):
    i = pl.program_id(1)
    halo = ATT_HALO_ROWS * GRID_W
    cur = ATT_ROWS * GRID_W
    kbuf[0:halo, :] = kp_ref[...]
    kbuf[halo:halo + cur, :] = kc_ref[...]
    kbuf[halo + cur:, :] = kn_ref[...]
    vbuf[0:halo, :] = vp_ref[...]
    vbuf[halo:halo + cur, :] = vc_ref[...]
    vbuf[halo + cur:, :] = vn_ref[...]

    row_blk = lax.broadcasted_iota(jnp.int32, (GROUP_W, GROUP_W), 0) // HEAD_DIM
    lane_blk = lax.broadcasted_iota(jnp.int32, (GROUP_W, GROUP_W), 1) // HEAD_DIM
    diag = row_blk == lane_blk
    low_half = lax.broadcasted_iota(jnp.int32, (GRID_W, 128), 1) < HEAD_DIM

    def row_body(rl, carry):
        r = i * ATT_ROWS + rl
        rs = jnp.clip(r - WIN_ROWS // 2, 0, n_rows - WIN_ROWS)
        cls = r - rs
        koff = pl.multiple_of((rs - i * ATT_ROWS + ATT_HALO_ROWS) * GRID_W, GRID_W)
        qoff = pl.multiple_of(rl * GRID_W, GRID_W)
        for g in range(N_HEAD_GROUPS):
            lanes = slice(g * GROUP_W, (g + 1) * GROUP_W)
            q_g = q_ref[pl.ds(qoff, GRID_W), lanes]
            q_bd = jnp.where(diag, jnp.concatenate([q_g] * HEADS_PER_GROUP, axis=0),
                             jnp.zeros((), jnp.bfloat16))
            k_w = kbuf[pl.ds(koff, WIN_KEYS), lanes]
            s = lax.dot_general(q_bd, k_w, (((1,), (1,)), ((), ())),
                                preferred_element_type=jnp.float32)
            s = s + bias_ref[cls, g]
            m = jnp.max(s, axis=-1, keepdims=True)
            e = jnp.exp2(s - m)
            inv_l = 1.0 / jnp.sum(e, axis=-1, keepdims=True)
            v_w = vbuf[pl.ds(koff, WIN_KEYS), lanes]
            o = jnp.dot(e.astype(jnp.bfloat16), v_w, preferred_element_type=jnp.float32)
            tiles = []
            for t2 in range(GROUP_W // 128):
                blk = []
                for hh in (2 * t2, 2 * t2 + 1):
                    rws = slice(hh * GRID_W, (hh + 1) * GRID_W)
                    blk.append(o[rws, t2 * 128:(t2 + 1) * 128] * inv_l[rws])
                tiles.append(jnp.where(low_half, blk[0], blk[1]))
            o_ref[pl.ds(qoff, GRID_W), lanes] = jnp.concatenate(tiles, axis=1).astype(jnp.bfloat16)
        return carry

    lax.fori_loop(0, ATT_ROWS, row_body, 0, unroll=True)


def _natten(q, k, v, bias):
    b, t, _ = q.shape
    n_rows = t // GRID_W
    cur = ATT_ROWS * GRID_W
    halo = ATT_HALO_ROWS * GRID_W
    per = ATT_ROWS // ATT_HALO_ROWS
    n_halo_blocks = t // halo
    cur_spec = pl.BlockSpec((None, cur, B_WIDTH), lambda bi, i: (bi, i, 0))
    prev_spec = pl.BlockSpec((None, halo, B_WIDTH),
                             lambda bi, i: (bi, jnp.maximum(i * per - 1, 0), 0))
    next_spec = pl.BlockSpec((None, halo, B_WIDTH),
                             lambda bi, i: (bi, jnp.minimum((i + 1) * per, n_halo_blocks - 1), 0))
    return pl.pallas_call(
        functools.partial(_natten_kernel, n_rows=n_rows),
        grid=(b, n_rows // ATT_ROWS),
        in_specs=[cur_spec, prev_spec, cur_spec, next_spec, prev_spec, cur_spec, next_spec,
                  _const_spec(bias.shape)],
        out_specs=cur_spec,
        out_shape=jax.ShapeDtypeStruct((b, t, B_WIDTH), jnp.bfloat16),
        scratch_shapes=[pltpu.VMEM((cur + 2 * halo, B_WIDTH), jnp.bfloat16),
                        pltpu.VMEM((cur + 2 * halo, B_WIDTH), jnp.bfloat16)],
        compiler_params=pltpu.CompilerParams(dimension_semantics=("arbitrary", "arbitrary"),
                                             vmem_limit_bytes=VMEM_LIMIT),
        name="natten",
    )(q, k, k, k, v, v, v, bias)


def _ffn_out_kernel(xp_ref, x_ref, xn_ref, yap_ref, ya_ref, yan_ref, ybp_ref, yb_ref, ybn_ref,
                    woa_ref, wob_ref, g2_ref, wa_ref, wb_ref, cw_ref, cb_ref, wd_ref, gf_ref,
                    o_ref, *, tiles_per_seq):
    tm = x_ref.shape[0]
    j = pl.program_id(0) % tiles_per_seq
    has_prev = j > 0
    has_next = j < tiles_per_seq - 1
    lo, hi = HALO_F32, HALO_F32 + tm

    def ext(prev_ref, cur_ref, next_ref):
        prev = prev_ref[...].astype(jnp.float32)
        nxt = next_ref[...].astype(jnp.float32)
        prev = jnp.where(has_prev, prev[prev.shape[0] - HALO_F32:], 0.0)
        nxt = jnp.where(has_next, nxt[:HALO_F32], 0.0)
        return jnp.concatenate([prev, cur_ref[...].astype(jnp.float32), nxt], axis=0)

    x_e = ext(xp_ref, x_ref, xn_ref)
    ya_e = ext(yap_ref, ya_ref, yan_ref).astype(jnp.bfloat16)
    yb_e = ext(ybp_ref, yb_ref, ybn_ref).astype(jnp.bfloat16)
    x1 = (x_e + jnp.dot(ya_e, woa_ref[...], preferred_element_type=jnp.float32)
          + jnp.dot(yb_e, wob_ref[...], preferred_element_type=jnp.float32))
    h2 = _rms(x1, g2_ref[...])
    a_e = jnp.dot(h2.astype(jnp.bfloat16), wa_ref[...], preferred_element_type=jnp.float32)
    b = jnp.dot(h2[lo:hi].astype(jnp.bfloat16), wb_ref[...], preferred_element_type=jnp.float32)
    n_ext = tm + 2 * HALO_F32
    a_prev = pltpu.roll(a_e, 1, axis=0)[lo:hi]
    a_next = pltpu.roll(a_e, n_ext - 1, axis=0)[lo:hi]
    a_conv = (cw_ref[0:1, :] * a_prev + cw_ref[1:2, :] * a_e[lo:hi] + cw_ref[2:3, :] * a_next
              + cb_ref[...])
    gated = (_gelu(a_conv) * b).astype(jnp.bfloat16)
    x2 = x1[lo:hi] + jnp.dot(gated, wd_ref[...], preferred_element_type=jnp.float32)
    o_ref[...] = _rms(x2, gf_ref[...])


def _ffn_out(x2d, ya, yb, p, seq_len):
    n = x2d.shape[0]
    tm = FFN_TM
    tiles_per_seq = seq_len // tm
    f32_per = tm // HALO_F32
    bf_per = tm // HALO_BF16

    def halo_specs(width, rows, per):
        last = n // rows - 1
        prev = pl.BlockSpec((rows, width), lambda i: (jnp.maximum(i * per - 1, 0), 0))
        cur = pl.BlockSpec((tm, width), lambda i: (i, 0))
        nxt = pl.BlockSpec((rows, width), lambda i: (jnp.minimum((i + 1) * per, last), 0))
        return [prev, cur, nxt]

    in_specs = (halo_specs(D_MODEL, HALO_F32, f32_per) + halo_specs(A_WIDTH, HALO_BF16, bf_per)
                + halo_specs(B_WIDTH, HALO_BF16, bf_per)
                + [_const_spec((A_WIDTH, D_MODEL)), _const_spec((B_WIDTH, D_MODEL), (1, 0)),
                   _const_spec((1, D_MODEL)), _const_spec((D_MODEL, D_FF)),
                   _const_spec((D_MODEL, D_FF), (0, 1)), _const_spec((3, D_FF)),
                   _const_spec((1, D_FF)),
                   _const_spec((D_FF, D_MODEL)), _const_spec((1, D_MODEL))])
    return pl.pallas_call(
        functools.partial(_ffn_out_kernel, tiles_per_seq=tiles_per_seq),
        grid=(n // tm,),
        in_specs=in_specs,
        out_specs=pl.BlockSpec((tm, D_MODEL), lambda i: (i, 0)),
        out_shape=jax.ShapeDtypeStruct((n, D_MODEL), jnp.float32),
        compiler_params=pltpu.CompilerParams(dimension_semantics=("arbitrary",),
                                             vmem_limit_bytes=VMEM_LIMIT),
        name="ffn_out",
    )(x2d, x2d, x2d, ya, ya, ya, yb, yb, yb,
      p["w_out"], p["w_out"], p["g2"], p["w_up"], p["w_up"], p["conv_w"], p["conv_b"],
      p["w_down"], p["gf"])


def _attention_bias(rpb):
    qc = np.arange(GRID_W)[:, None]
    kc = np.arange(GRID_W)[None, :]
    win_start = np.clip(qc - WIN_COLS // 2, 0, GRID_W - WIN_COLS)
    valid = (kc >= win_start) & (kc < win_start + WIN_COLS)
    dc = np.clip(kc - qc, -(WIN_COLS - 1), WIN_COLS - 1) + WIN_COLS - 1
    onehot = ((dc[None] == np.arange(2 * WIN_COLS - 1)[:, None, None]) & valid[None])
    t = jnp.einsum("hdj,jqk->hqdk", rpb * LOG2E, jnp.asarray(onehot, jnp.float32),
                   precision=lax.Precision.HIGHEST)
    t = t + jnp.asarray(np.where(valid, 0.0, NEG_INF), jnp.float32)[None, :, None, :]
    shape = (N_HEAD_GROUPS, HEADS_PER_GROUP * GRID_W, WIN_KEYS)
    return jnp.stack([t[:, :, WIN_ROWS - 1 - c:2 * WIN_ROWS - 1 - c, :].reshape(shape)
                      for c in range(WIN_ROWS)])


def _prepare(norm1_g, w_in, gate_norm_g, w_spatial, b_spatial, rpb, w_out, norm2_g, w_up,
             conv_w, conv_b, w_down, final_norm_g):
    bf = jnp.bfloat16
    return {
        "g1": norm1_g[0][None, :],
        "w_in": w_in[0].astype(bf),
        "gate_g": gate_norm_g[0][None, :],
        "w_s": w_spatial[0].astype(bf),
        "b_s": jnp.repeat(b_spatial[0].T, HEAD_DIM, axis=1),
        "bias": _attention_bias(rpb[0]),
        "w_out": w_out[0].astype(bf),
        "g2": norm2_g[0][None, :],
        "w_up": w_up[0].astype(bf),
        "conv_w": conv_w[0],
        "conv_b": conv_b[0][None, :],
        "w_down": w_down[0].astype(bf),
        "gf": final_norm_g[None, :],
    }


def _trunk(x, p):
    b, t, d = x.shape
    x2d = x.reshape(b * t, d)
    ya, q, k, v = _mix_in(x2d, p)
    shp = (b, t, B_WIDTH)
    yb = _natten(q.reshape(shp), k.reshape(shp), v.reshape(shp), p["bias"])
    y = _ffn_out(x2d, ya, yb.reshape(b * t, B_WIDTH), p, t)
    return y.reshape(b, t, d)


def kernel(x_prompt, x_sample, norm1_g, w_in, gate_norm_g, w_spatial, b_spatial, rpb, w_out,
           norm2_g, w_up, conv_w, conv_b, w_down, final_norm_g):
    p = _prepare(norm1_g, w_in, gate_norm_g, w_spatial, b_spatial, rpb, w_out, norm2_g, w_up,
                 conv_w, conv_b, w_down, final_norm_g)
    return (_trunk(x_prompt, p), _trunk(x_sample, p))
```

```python
import functools
import math

import numpy as np
import jax
import jax.numpy as jnp
from jax import lax
from jax.experimental import pallas as pl
from jax.experimental.pallas import tpu as pltpu

D_MODEL = 1024
GRID_W = 64
A_WIDTH = 512
B_WIDTH = 512
HEAD_DIM = 64
A_GROUPS = 8
B_HEADS = 8
IN_WIDTH = 2 * A_WIDTH + 3 * B_WIDTH
CHUNK = 128
WIN_ROWS = 8
WIN_COLS = 16
D_FF = 2816
EPS = 1e-6
NEG_INF = -1e30
SQRT_HALF = math.sqrt(0.5)
LOG2E = math.log2(math.e)

HEADS_PER_GROUP = 4
GROUP_W = HEADS_PER_GROUP * HEAD_DIM
N_HEAD_GROUPS = B_HEADS // HEADS_PER_GROUP
WIN_KEYS = WIN_ROWS * GRID_W

MIX_TM = 1024
ATT_ROWS = 32
ATT_HALO_ROWS = 8
ATT_WIN_ROWS = ATT_ROWS + 2 * ATT_HALO_ROWS
FFN_TM = 1024
HALO_F32 = 8
HALO_BF16 = 16
VMEM_LIMIT = 62 * 1024 * 1024


def _gelu(x):
    return 0.5 * x * (1.0 + lax.erf(x * SQRT_HALF))


def _rms(x, g):
    return x * lax.rsqrt(jnp.mean(x * x, axis=-1, keepdims=True) + EPS) * g


def _const_spec(shape, block_index=None):
    idx = (0,) * len(shape) if block_index is None else block_index
    return pl.BlockSpec(shape, lambda *_: idx, pipeline_mode=pl.Buffered(1))


def _mix_in_kernel(x_ref, g1_ref, win_ref, gg_ref, ws_ref, bs_ref,
                   ya_ref, q_ref, k_ref, v_ref):
    tm = x_ref.shape[0]
    h = _rms(x_ref[...], g1_ref[...]).astype(jnp.bfloat16)
    z = jnp.dot(h, win_ref[...], preferred_element_type=jnp.float32)
    u = _gelu(z[:, :A_WIDTH])
    va = _rms(_gelu(z[:, A_WIDTH:2 * A_WIDTH]), gg_ref[...]).astype(jnp.bfloat16)
    q_ref[...] = (z[:, 2 * A_WIDTH:2 * A_WIDTH + B_WIDTH] * (HEAD_DIM ** -0.5 * LOG2E)).astype(jnp.bfloat16)
    k_ref[...] = z[:, 2 * A_WIDTH + B_WIDTH:2 * A_WIDTH + 2 * B_WIDTH].astype(jnp.bfloat16)
    v_ref[...] = z[:, 2 * A_WIDTH + 2 * B_WIDTH:].astype(jnp.bfloat16)

    first_half = lax.broadcasted_iota(jnp.int32, (CHUNK, 128), 1) < HEAD_DIM
    for c in range(tm // CHUNK):
        rows = slice(c * CHUNK, (c + 1) * CHUNK)
        for j in range(A_WIDTH // 128):
            cols = slice(j * 128, (j + 1) * 128)
            va_t = va[rows, cols]
            s0 = jnp.dot(ws_ref[2 * j], va_t, preferred_element_type=jnp.float32)
            s1 = jnp.dot(ws_ref[2 * j + 1], va_t, preferred_element_type=jnp.float32)
            s = jnp.where(first_half, s0, s1) + bs_ref[:, cols]
            ya_ref[rows, cols] = (u[rows, cols] * s).astype(jnp.bfloat16)


def _mix_in(x2d, p):
    n = x2d.shape[0]
    tm = MIX_TM
    tok = lambda w: pl.BlockSpec((tm, w), lambda i: (i, 0))
    out = jax.ShapeDtypeStruct((n, A_WIDTH), jnp.bfloat16)
    return pl.pallas_call(
        _mix_in_kernel,
        grid=(n // tm,),
        in_specs=[tok(D_MODEL), _const_spec((1, D_MODEL)), _const_spec((D_MODEL, IN_WIDTH)),
                  _const_spec((1, A_WIDTH)), _const_spec((A_GROUPS, CHUNK, CHUNK)),
                  _const_spec((CHUNK, A_WIDTH))],
        out_specs=[tok(A_WIDTH)] * 4,
        out_shape=[out] * 4,
        compiler_params=pltpu.CompilerParams(dimension_semantics=("arbitrary",),
                                             vmem_limit_bytes=VMEM_LIMIT),
        name="mix_in",
    )(x2d, p["g1"], p["w_in"], p["gate_g"], p["w_s"], p["b_s"])


def _kv_first_row(i, n_rows):
    return jnp.clip(i * ATT_ROWS - ATT_HALO_ROWS, 0, n_rows - ATT_WIN_ROWS)


def _natten_kernel(q_ref, k_ref, v_ref, bias_ref, o_ref, *, n_rows):
    i = pl.program_id(1)
    first_row = _kv_first_row(i, n_rows)

    row_blk = lax.broadcasted_iota(jnp.int32, (GROUP_W, GROUP_W), 0) // HEAD_DIM
    lane_blk = lax.broadcasted_iota(jnp.int32, (GROUP_W, GROUP_W), 1) // HEAD_DIM
    diag = row_blk == lane_blk
    low_half = lax.broadcasted_iota(jnp.int32, (GRID_W, 128), 1) < HEAD_DIM

    def row_body(rl, carry):
        r = i * ATT_ROWS + rl
        rs = jnp.clip(r - WIN_ROWS // 2, 0, n_rows - WIN_ROWS)
        cls = r - rs
        bias_copy = (cls + 1) % 2
        bias_tile = (WIN_ROWS - cls) // 2
        koff = pl.multiple_of((rs - first_row) * GRID_W, GRID_W)
        qoff = pl.multiple_of(rl * GRID_W, GRID_W)
        for g in range(N_HEAD_GROUPS):
            lanes = slice(g * GROUP_W, (g + 1) * GROUP_W)
            q_g = q_ref[pl.ds(qoff, GRID_W), lanes]
            q_bd = jnp.where(diag, jnp.concatenate([q_g] * HEADS_PER_GROUP, axis=0),
                             jnp.zeros((), jnp.bfloat16))
            k_w = k_ref[pl.ds(koff, WIN_KEYS), lanes]
            s = lax.dot_general(q_bd, k_w, (((1,), (1,)), ((), ())),
                                preferred_element_type=jnp.float32)
            s = s + jnp.concatenate([bias_ref[bias_copy, g, bias_tile + jt]
                                     for jt in range(WIN_KEYS // 128)], axis=1)
            m = jnp.max(s, axis=-1, keepdims=True)
            e = jnp.exp2(s - m)
            inv_l = 1.0 / jnp.sum(e, axis=-1, keepdims=True)
            v_w = v_ref[pl.ds(koff, WIN_KEYS), lanes]
            o = jnp.dot(e.astype(jnp.bfloat16), v_w, preferred_element_type=jnp.float32)
            tiles = []
            for t2 in range(GROUP_W // 128):
                blk = []
                for hh in (2 * t2, 2 * t2 + 1):
                    rws = slice(hh * GRID_W, (hh + 1) * GRID_W)
                    blk.append(o[rws, t2 * 128:(t2 + 1) * 128] * inv_l[rws])
                tiles.append(jnp.where(low_half, blk[0], blk[1]))
            o_ref[pl.ds(qoff, GRID_W), lanes] = jnp.concatenate(tiles, axis=1).astype(jnp.bfloat16)
        return carry

    lax.fori_loop(0, ATT_ROWS, row_body, 0, unroll=True)


def _natten(q, k, v, bias):
    b, t, _ = q.shape
    n_rows = t // GRID_W
    q_spec = pl.BlockSpec((None, ATT_ROWS * GRID_W, B_WIDTH), lambda bi, i: (bi, i, 0))
    kv_spec = pl.BlockSpec((None, pl.Element(ATT_WIN_ROWS * GRID_W), pl.Element(B_WIDTH)),
                           lambda bi, i: (bi, _kv_first_row(i, n_rows) * GRID_W, 0))
    return pl.pallas_call(
        functools.partial(_natten_kernel, n_rows=n_rows),
        grid=(b, n_rows // ATT_ROWS),
        in_specs=[q_spec, kv_spec, kv_spec, _const_spec(bias.shape)],
        out_specs=q_spec,
        out_shape=jax.ShapeDtypeStruct((b, t, B_WIDTH), jnp.bfloat16),
        compiler_params=pltpu.CompilerParams(dimension_semantics=("arbitrary", "arbitrary"),
                                             vmem_limit_bytes=VMEM_LIMIT),
        name="natten",
    )(q, k, v, bias)


def _ffn_out_kernel(xp_ref, x_ref, xn_ref, yap_ref, ya_ref, yan_ref, ybp_ref, yb_ref, ybn_ref,
                    woa_ref, wob_ref, g2_ref, wa_ref, wb_ref, cw_ref, cb_ref, wd_ref, gf_ref,
                    o_ref, *, tiles_per_seq):
    tm = x_ref.shape[0]
    j = pl.program_id(0) % tiles_per_seq
    has_prev = j > 0
    has_next = j < tiles_per_seq - 1
    lo, hi = HALO_F32, HALO_F32 + tm

    def ext(prev_ref, cur_ref, next_ref):
        prev = prev_ref[...].astype(jnp.float32)
        nxt = next_ref[...].astype(jnp.float32)
        prev = jnp.where(has_prev, prev[prev.shape[0] - HALO_F32:], 0.0)
        nxt = jnp.where(has_next, nxt[:HALO_F32], 0.0)
        return jnp.concatenate([prev, cur_ref[...].astype(jnp.float32), nxt], axis=0)

    x_e = ext(xp_ref, x_ref, xn_ref)
    ya_e = ext(yap_ref, ya_ref, yan_ref).astype(jnp.bfloat16)
    yb_e = ext(ybp_ref, yb_ref, ybn_ref).astype(jnp.bfloat16)
    x1 = (x_e + jnp.dot(ya_e, woa_ref[...], preferred_element_type=jnp.float32)
          + jnp.dot(yb_e, wob_ref[...], preferred_element_type=jnp.float32))
    h2 = _rms(x1, g2_ref[...])
    a_e = jnp.dot(h2.astype(jnp.bfloat16), wa_ref[...], preferred_element_type=jnp.float32)
    b = jnp.dot(h2[lo:hi].astype(jnp.bfloat16), wb_ref[...], preferred_element_type=jnp.float32)
    n_ext = tm + 2 * HALO_F32
    a_prev = pltpu.roll(a_e, 1, axis=0)[lo:hi]
    a_next = pltpu.roll(a_e, n_ext - 1, axis=0)[lo:hi]
    a_conv = (cw_ref[0:1, :] * a_prev + cw_ref[1:2, :] * a_e[lo:hi] + cw_ref[2:3, :] * a_next
              + cb_ref[...])
    gated = (_gelu(a_conv) * b).astype(jnp.bfloat16)
    x2 = x1[lo:hi] + jnp.dot(gated, wd_ref[...], preferred_element_type=jnp.float32)
    o_ref[...] = _rms(x2, gf_ref[...])


def _ffn_out(x2d, ya, yb, p, seq_len):
    n = x2d.shape[0]
    tm = FFN_TM
    tiles_per_seq = seq_len // tm
    f32_per = tm // HALO_F32
    bf_per = tm // HALO_BF16

    def halo_specs(width, rows, per):
        last = n // rows - 1
        prev = pl.BlockSpec((rows, width), lambda i: (jnp.maximum(i * per - 1, 0), 0))
        cur = pl.BlockSpec((tm, width), lambda i: (i, 0))
        nxt = pl.BlockSpec((rows, width), lambda i: (jnp.minimum((i + 1) * per, last), 0))
        return [prev, cur, nxt]

    in_specs = (halo_specs(D_MODEL, HALO_F32, f32_per) + halo_specs(A_WIDTH, HALO_BF16, bf_per)
                + halo_specs(B_WIDTH, HALO_BF16, bf_per)
                + [_const_spec((A_WIDTH, D_MODEL)), _const_spec((B_WIDTH, D_MODEL), (1, 0)),
                   _const_spec((1, D_MODEL)), _const_spec((D_MODEL, D_FF)),
                   _const_spec((D_MODEL, D_FF), (0, 1)), _const_spec((3, D_FF)),
                   _const_spec((1, D_FF)),
                   _const_spec((D_FF, D_MODEL)), _const_spec((1, D_MODEL))])
    return pl.pallas_call(
        functools.partial(_ffn_out_kernel, tiles_per_seq=tiles_per_seq),
        grid=(n // tm,),
        in_specs=in_specs,
        out_specs=pl.BlockSpec((tm, D_MODEL), lambda i: (i, 0)),
        out_shape=jax.ShapeDtypeStruct((n, D_MODEL), jnp.float32),
        compiler_params=pltpu.CompilerParams(dimension_semantics=("arbitrary",),
                                             vmem_limit_bytes=VMEM_LIMIT),
        name="ffn_out",
    )(x2d, x2d, x2d, ya, ya, ya, yb, yb, yb,
      p["w_out"], p["w_out"], p["g2"], p["w_up"], p["w_up"], p["conv_w"], p["conv_b"],
      p["w_down"], p["gf"])


def _attention_bias(rpb):
    qc = np.arange(GRID_W)[:, None]
    kc = np.arange(GRID_W)[None, :]
    win_start = np.clip(qc - WIN_COLS // 2, 0, GRID_W - WIN_COLS)
    valid = (kc >= win_start) & (kc < win_start + WIN_COLS)
    dc = np.clip(kc - qc, -(WIN_COLS - 1), WIN_COLS - 1) + WIN_COLS - 1
    onehot = ((dc[None] == np.arange(2 * WIN_COLS - 1)[:, None, None]) & valid[None])
    t = jnp.einsum("hdj,jqk->hqdk", rpb * LOG2E, jnp.asarray(onehot, jnp.float32),
                   precision=lax.Precision.HIGHEST)
    t = t + jnp.asarray(np.where(valid, 0.0, NEG_INF), jnp.float32)[None, :, None, :]
    t = t.reshape(N_HEAD_GROUPS, HEADS_PER_GROUP * GRID_W, 2 * WIN_ROWS - 1, GRID_W)

    def tiles(pad):
        padded = jnp.pad(t, ((0, 0), (0, 0), pad, (0, 0)))
        padded = padded.reshape(N_HEAD_GROUPS, HEADS_PER_GROUP * GRID_W, WIN_ROWS, 128)
        return jnp.transpose(padded, (0, 2, 1, 3))

    return jnp.stack([tiles((0, 1)), tiles((1, 0))])


def _prepare(norm1_g, w_in, gate_norm_g, w_spatial, b_spatial, rpb, w_out, norm2_g, w_up,
             conv_w, conv_b, w_down, final_norm_g):
    bf = jnp.bfloat16
    return {
        "g1": norm1_g[0][None, :],
        "w_in": w_in[0].astype(bf),
        "gate_g": gate_norm_g[0][None, :],
        "w_s": w_spatial[0].astype(bf),
        "b_s": jnp.repeat(b_spatial[0].T, HEAD_DIM, axis=1),
        "bias": _attention_bias(rpb[0]),
        "w_out": w_out[0].astype(bf),
        "g2": norm2_g[0][None, :],
        "w_up": w_up[0].astype(bf),
        "conv_w": conv_w[0],
        "conv_b": conv_b[0][None, :],
        "w_down": w_down[0].astype(bf),
        "gf": final_norm_g[None, :],
    }


def _trunk(x, p):
    b, t, d = x.shape
    x2d = x.reshape(b * t, d)
    ya, q, k, v = _mix_in(x2d, p)
    shp = (b, t, B_WIDTH)
    yb = _natten(q.reshape(shp), k.reshape(shp), v.reshape(shp), p["bias"])
    y = _ffn_out(x2d, ya, yb.reshape(b * t, B_WIDTH), p, t)
    return y.reshape(b, t, d)


def kernel(x_prompt, x_sample, norm1_g, w_in, gate_norm_g, w_spatial, b_spatial, rpb, w_out,
           norm2_g, w_up, conv_w, conv_b, w_down, final_norm_g):
    p = _prepare(norm1_g, w_in, gate_norm_g, w_spatial, b_spatial, rpb, w_out, norm2_g, w_up,
                 conv_w, conv_b, w_down, final_norm_g)
    return (_trunk(x_prompt, p), _trunk(x_sample, p))
```

```python
import functools
import math

import jax
import jax.numpy as jnp
from jax import lax
from jax.experimental import pallas as pl
from jax.experimental.pallas import tpu as pltpu

D_MODEL = 1024
GRID_W = 64
A_WIDTH = 512
B_WIDTH = 512
HEAD_DIM = 64
A_GROUPS = 8
B_HEADS = 8
IN_WIDTH = 2 * A_WIDTH + 3 * B_WIDTH
CHUNK = 128
WIN_ROWS = 8
WIN_COLS = 16
D_FF = 2816
EPS = 1e-6
NEG_INF = -1e30
SQRT_HALF = math.sqrt(0.5)
LOG2E = math.log2(math.e)

HEADS_PER_GROUP = 4
GROUP_W = HEADS_PER_GROUP * HEAD_DIM
N_HEAD_GROUPS = B_HEADS // HEADS_PER_GROUP
WIN_KEYS = WIN_ROWS * GRID_W

MIX_TM = 1024
ATT_ROWS = 32
ATT_HALO_ROWS = 8
ATT_WIN_ROWS = ATT_ROWS + 2 * ATT_HALO_ROWS
FFN_TM = 1024
HALO_F32 = 8
HALO_BF16 = 16
VMEM_LIMIT = 62 * 1024 * 1024


def _gelu(x):
    return 0.5 * x * (1.0 + lax.erf(x * SQRT_HALF))


def _rms(x, g):
    return x * lax.rsqrt(jnp.mean(x * x, axis=-1, keepdims=True) + EPS) * g


def _const_spec(shape, block_index=None):
    idx = (0,) * len(shape) if block_index is None else block_index
    return pl.BlockSpec(shape, lambda *_: idx, pipeline_mode=pl.Buffered(1))


def _bias_tiles_kernel(rpb_ref, o_ref):
    n_dr = 2 * WIN_ROWS - 1
    q_col = lax.broadcasted_iota(jnp.int32, (GRID_W, 128), 0)
    lane = lax.broadcasted_iota(jnp.int32, (GRID_W, 128), 1)
    k_col = lane % GRID_W
    win_start = jnp.clip(q_col - WIN_COLS // 2, 0, GRID_W - WIN_COLS)
    valid = (k_col >= win_start) & (k_col < win_start + WIN_COLS)
    low_half = lane < GRID_W

    def toeplitz(h, dr, lane_offset):
        if not 0 <= dr < n_dr:
            return jnp.zeros((GRID_W, 128), jnp.float32)
        row = jnp.broadcast_to(rpb_ref[h, dr:dr + 1, :], (GRID_W, 128))
        shift = (lane_offset - (WIN_COLS - 1)) % 128
        return pltpu.roll(row, shift, axis=1, stride=1, stride_axis=0)

    for copy in range(2):
        for h in range(B_HEADS):
            g, hh = divmod(h, HEADS_PER_GROUP)
            for t in range(WIN_ROWS):
                dr_lo = 2 * t - copy
                tile = jnp.where(low_half, toeplitz(h, dr_lo, 0), toeplitz(h, dr_lo + 1, GRID_W))
                o_ref[copy, g, t, hh * GRID_W:(hh + 1) * GRID_W, :] = jnp.where(
                    valid, tile * LOG2E, NEG_INF)


def _bias_tiles(rpb):
    rpb_pad = jnp.pad(rpb, ((0, 0), (0, 0), (0, 128 - rpb.shape[-1])))
    shape = (2, N_HEAD_GROUPS, WIN_ROWS, HEADS_PER_GROUP * GRID_W, 128)
    return pl.pallas_call(
        _bias_tiles_kernel,
        out_shape=jax.ShapeDtypeStruct(shape, jnp.float32),
        name="bias_tiles",
    )(rpb_pad)


def _mix_in_kernel(x_ref, g1_ref, win_ref, gg_ref, ws_ref, bs_ref,
                   ya_ref, q_ref, k_ref, v_ref):
    tm = x_ref.shape[0]
    h = _rms(x_ref[...], g1_ref[...]).astype(jnp.bfloat16)
    z = jnp.dot(h, win_ref[...], preferred_element_type=jnp.float32)
    u = _gelu(z[:, :A_WIDTH])
    va = _rms(_gelu(z[:, A_WIDTH:2 * A_WIDTH]), gg_ref[...]).astype(jnp.bfloat16)
    q_scale = HEAD_DIM ** -0.5 * LOG2E
    q_ref[...] = (z[:, 2 * A_WIDTH:2 * A_WIDTH + B_WIDTH] * q_scale).astype(jnp.bfloat16)
    k_ref[...] = z[:, 2 * A_WIDTH + B_WIDTH:2 * A_WIDTH + 2 * B_WIDTH].astype(jnp.bfloat16)
    v_ref[...] = z[:, 2 * A_WIDTH + 2 * B_WIDTH:].astype(jnp.bfloat16)

    first_half = lax.broadcasted_iota(jnp.int32, (CHUNK, 128), 1) < HEAD_DIM
    for c in range(tm // CHUNK):
        rows = slice(c * CHUNK, (c + 1) * CHUNK)
        for j in range(A_WIDTH // 128):
            cols = slice(j * 128, (j + 1) * 128)
            va_t = va[rows, cols]
            s0 = jnp.dot(ws_ref[2 * j], va_t, preferred_element_type=jnp.float32)
            s1 = jnp.dot(ws_ref[2 * j + 1], va_t, preferred_element_type=jnp.float32)
            s = jnp.where(first_half, s0, s1) + bs_ref[:, cols]
            ya_ref[rows, cols] = (u[rows, cols] * s).astype(jnp.bfloat16)


def _mix_in(x2d, p):
    n = x2d.shape[0]
    tm = MIX_TM
    tok = lambda w: pl.BlockSpec((tm, w), lambda i: (i, 0))
    out = jax.ShapeDtypeStruct((n, A_WIDTH), jnp.bfloat16)
    return pl.pallas_call(
        _mix_in_kernel,
        grid=(n // tm,),
        in_specs=[tok(D_MODEL), _const_spec((1, D_MODEL)), _const_spec((D_MODEL, IN_WIDTH)),
                  _const_spec((1, A_WIDTH)), _const_spec((A_GROUPS, CHUNK, CHUNK)),
                  _const_spec((CHUNK, A_WIDTH))],
        out_specs=[tok(A_WIDTH)] * 4,
        out_shape=[out] * 4,
        compiler_params=pltpu.CompilerParams(dimension_semantics=("arbitrary",),
                                             vmem_limit_bytes=VMEM_LIMIT),
        name="mix_in",
    )(x2d, p["g1"], p["w_in"], p["gate_g"], p["w_s"], p["b_s"])


def _kv_first_row(i, n_rows):
    return jnp.clip(i * ATT_ROWS - ATT_HALO_ROWS, 0, n_rows - ATT_WIN_ROWS)


def _natten_kernel(q_ref, k_ref, v_ref, bias_ref, o_ref, *, n_rows):
    i = pl.program_id(1)
    first_row = _kv_first_row(i, n_rows)

    row_blk = lax.broadcasted_iota(jnp.int32, (GROUP_W, GROUP_W), 0) // HEAD_DIM
    lane_blk = lax.broadcasted_iota(jnp.int32, (GROUP_W, GROUP_W), 1) // HEAD_DIM
    diag = row_blk == lane_blk
    low_half = lax.broadcasted_iota(jnp.int32, (GRID_W, 128), 1) < HEAD_DIM

    def row_body(rl, carry):
        r = i * ATT_ROWS + rl
        rs = jnp.clip(r - WIN_ROWS // 2, 0, n_rows - WIN_ROWS)
        cls = r - rs
        bias_copy = (cls + 1) % 2
        bias_tile = (WIN_ROWS - cls) // 2
        koff = pl.multiple_of((rs - first_row) * GRID_W, GRID_W)
        qoff = pl.multiple_of(rl * GRID_W, GRID_W)
        for g in range(N_HEAD_GROUPS):
            lanes = slice(g * GROUP_W, (g + 1) * GROUP_W)
            q_g = q_ref[pl.ds(qoff, GRID_W), lanes]
            q_bd = jnp.where(diag, jnp.concatenate([q_g] * HEADS_PER_GROUP, axis=0),
                             jnp.zeros((), jnp.bfloat16))
            k_w = k_ref[pl.ds(koff, WIN_KEYS), lanes]
            s = lax.dot_general(q_bd, k_w, (((1,), (1,)), ((), ())),
                                preferred_element_type=jnp.float32)
            s = s + jnp.concatenate([bias_ref[bias_copy, g, bias_tile + jt]
                                     for jt in range(WIN_KEYS // 128)], axis=1)
            m = jnp.max(s, axis=-1, keepdims=True)
            e = jnp.exp2(s - m)
            inv_l = 1.0 / jnp.sum(e, axis=-1, keepdims=True)
            v_w = v_ref[pl.ds(koff, WIN_KEYS), lanes]
            o = jnp.dot(e.astype(jnp.bfloat16), v_w, preferred_element_type=jnp.float32)
            tiles = []
            for t2 in range(GROUP_W // 128):
                blk = []
                for hh in (2 * t2, 2 * t2 + 1):
                    rws = slice(hh * GRID_W, (hh + 1) * GRID_W)
                    blk.append(o[rws, t2 * 128:(t2 + 1) * 128] * inv_l[rws])
                tiles.append(jnp.where(low_half, blk[0], blk[1]))
            o_ref[pl.ds(qoff, GRID_W), lanes] = jnp.concatenate(tiles, axis=1).astype(jnp.bfloat16)
        return carry

    lax.fori_loop(0, ATT_ROWS, row_body, 0, unroll=True)


def _natten(q, k, v, bias):
    b, t, _ = q.shape
    n_rows = t // GRID_W
    q_spec = pl.BlockSpec((None, ATT_ROWS * GRID_W, B_WIDTH), lambda bi, i: (bi, i, 0))
    kv_spec = pl.BlockSpec((None, pl.Element(ATT_WIN_ROWS * GRID_W), pl.Element(B_WIDTH)),
                           lambda bi, i: (bi, _kv_first_row(i, n_rows) * GRID_W, 0))
    return pl.pallas_call(
        functools.partial(_natten_kernel, n_rows=n_rows),
        grid=(b, n_rows // ATT_ROWS),
        in_specs=[q_spec, kv_spec, kv_spec, _const_spec(bias.shape)],
        out_specs=q_spec,
        out_shape=jax.ShapeDtypeStruct((b, t, B_WIDTH), jnp.bfloat16),
        compiler_params=pltpu.CompilerParams(dimension_semantics=("arbitrary", "arbitrary"),
                                             vmem_limit_bytes=VMEM_LIMIT),
        name="natten",
    )(q, k, v, bias)


def _ffn_out_kernel(xp_ref, x_ref, xn_ref, yap_ref, ya_ref, yan_ref, ybp_ref, yb_ref, ybn_ref,
                    woa_ref, wob_ref, g2_ref, wa_ref, wb_ref, cw_ref, cb_ref, wd_ref, gf_ref,
                    o_ref, *, tiles_per_seq):
    tm = x_ref.shape[0]
    j = pl.program_id(0) % tiles_per_seq
    has_prev = j > 0
    has_next = j < tiles_per_seq - 1
    lo, hi = HALO_F32, HALO_F32 + tm

    def ext(prev_ref, cur_ref, next_ref):
        prev = prev_ref[...].astype(jnp.float32)
        nxt = next_ref[...].astype(jnp.float32)
        prev = jnp.where(has_prev, prev[prev.shape[0] - HALO_F32:], 0.0)
        nxt = jnp.where(has_next, nxt[:HALO_F32], 0.0)
        return jnp.concatenate([prev, cur_ref[...].astype(jnp.float32), nxt], axis=0)

    x_e = ext(xp_ref, x_ref, xn_ref)
    ya_e = ext(yap_ref, ya_ref, yan_ref).astype(jnp.bfloat16)
    yb_e = ext(ybp_ref, yb_ref, ybn_ref).astype(jnp.bfloat16)
    x1 = (x_e + jnp.dot(ya_e, woa_ref[...], preferred_element_type=jnp.float32)
          + jnp.dot(yb_e, wob_ref[...], preferred_element_type=jnp.float32))
    h2 = _rms(x1, g2_ref[...])
    a_e = jnp.dot(h2.astype(jnp.bfloat16), wa_ref[...], preferred_element_type=jnp.float32)
    b = jnp.dot(h2[lo:hi].astype(jnp.bfloat16), wb_ref[...], preferred_element_type=jnp.float32)
    n_ext = tm + 2 * HALO_F32
    a_prev = pltpu.roll(a_e, 1, axis=0)[lo:hi]
    a_next = pltpu.roll(a_e, n_ext - 1, axis=0)[lo:hi]
    a_conv = (cw_ref[0:1, :] * a_prev + cw_ref[1:2, :] * a_e[lo:hi] + cw_ref[2:3, :] * a_next
              + cb_ref[...])
    gated = (_gelu(a_conv) * b).astype(jnp.bfloat16)
    x2 = x1[lo:hi] + jnp.dot(gated, wd_ref[...], preferred_element_type=jnp.float32)
    o_ref[...] = _rms(x2, gf_ref[...])


def _ffn_out(x2d, ya, yb, p, seq_len):
    n = x2d.shape[0]
    tm = FFN_TM
    tiles_per_seq = seq_len // tm
    f32_per = tm // HALO_F32
    bf_per = tm // HALO_BF16

    def halo_specs(width, rows, per):
        last = n // rows - 1
        prev = pl.BlockSpec((rows, width), lambda i: (jnp.maximum(i * per - 1, 0), 0))
        cur = pl.BlockSpec((tm, width), lambda i: (i, 0))
        nxt = pl.BlockSpec((rows, width), lambda i: (jnp.minimum((i + 1) * per, last), 0))
        return [prev, cur, nxt]

    in_specs = (halo_specs(D_MODEL, HALO_F32, f32_per) + halo_specs(A_WIDTH, HALO_BF16, bf_per)
                + halo_specs(B_WIDTH, HALO_BF16, bf_per)
                + [_const_spec((A_WIDTH, D_MODEL)), _const_spec((B_WIDTH, D_MODEL), (1, 0)),
                   _const_spec((1, D_MODEL)), _const_spec((D_MODEL, D_FF)),
                   _const_spec((D_MODEL, D_FF), (0, 1)), _const_spec((3, D_FF)),
                   _const_spec((1, D_FF)),
                   _const_spec((D_FF, D_MODEL)), _const_spec((1, D_MODEL))])
    return pl.pallas_call(
        functools.partial(_ffn_out_kernel, tiles_per_seq=tiles_per_seq),
        grid=(n // tm,),
        in_specs=in_specs,
        out_specs=pl.BlockSpec((tm, D_MODEL), lambda i: (i, 0)),
        out_shape=jax.ShapeDtypeStruct((n, D_MODEL), jnp.float32),
        compiler_params=pltpu.CompilerParams(dimension_semantics=("arbitrary",),
                                             vmem_limit_bytes=VMEM_LIMIT),
        name="ffn_out",
    )(x2d, x2d, x2d, ya, ya, ya, yb, yb, yb,
      p["w_out"], p["w_out"], p["g2"], p["w_up"], p["w_up"], p["conv_w"], p["conv_b"],
      p["w_down"], p["gf"])


def _prepare(norm1_g, w_in, gate_norm_g, w_spatial, b_spatial, rpb, w_out, norm2_g, w_up,
             conv_w, conv_b, w_down, final_norm_g):
    bf = jnp.bfloat16
    return {
        "g1": norm1_g[0][None, :],
        "w_in": w_in[0].astype(bf),
        "gate_g": gate_norm_g[0][None, :],
        "w_s": w_spatial[0].astype(bf),
        "b_s": jnp.repeat(b_spatial[0].T, HEAD_DIM, axis=1),
        "bias": _bias_tiles(rpb[0]),
        "w_out": w_out[0].astype(bf),
        "g2": norm2_g[0][None, :],
        "w_up": w_up[0].astype(bf),
        "conv_w": conv_w[0],
        "conv_b": conv_b[0][None, :],
        "w_down": w_down[0].astype(bf),
        "gf": final_norm_g[None, :],
    }


def _trunk(x, p):
    b, t, d = x.shape
    x2d = x.reshape(b * t, d)
    ya, q, k, v = _mix_in(x2d, p)
    shp = (b, t, B_WIDTH)
    yb = _natten(q.reshape(shp), k.reshape(shp), v.reshape(shp), p["bias"])
    y = _ffn_out(x2d, ya, yb.reshape(b * t, B_WIDTH), p, t)
    return y.reshape(b, t, d)


def kernel(x_prompt, x_sample, norm1_g, w_in, gate_norm_g, w_spatial, b_spatial, rpb, w_out,
           norm2_g, w_up, conv_w, conv_b, w_down, final_norm_g):
    p = _prepare(norm1_g, w_in, gate_norm_g, w_spatial, b_spatial, rpb, w_out, norm2_g, w_up,
                 conv_w, conv_b, w_down, final_norm_g)
    return (_trunk(x_prompt, p), _trunk(x_sample, p))
```

```python
import functools
import math

import jax
import jax.numpy as jnp
from jax import lax
from jax.experimental import pallas as pl
from jax.experimental.pallas import tpu as pltpu

D_MODEL = 1024
GRID_W = 64
A_WIDTH = 512
B_WIDTH = 512
HEAD_DIM = 64
A_GROUPS = 8
B_HEADS = 8
IN_WIDTH = 2 * A_WIDTH + 3 * B_WIDTH
CHUNK = 128
WIN_ROWS = 8
WIN_COLS = 16
D_FF = 2816
EPS = 1e-6
NEG_INF = -1e30
SQRT_HALF = math.sqrt(0.5)
LOG2E = math.log2(math.e)

HEADS_PER_GROUP = 4
GROUP_W = HEADS_PER_GROUP * HEAD_DIM
N_HEAD_GROUPS = B_HEADS // HEADS_PER_GROUP
WIN_KEYS = WIN_ROWS * GRID_W

MIX_TM = 1024
ATT_ROWS = 32
ATT_HALO_ROWS = 8
ATT_WIN_ROWS = ATT_ROWS + 2 * ATT_HALO_ROWS
FFN_TM = 1024
HALO_F32 = 8
HALO_BF16 = 16
VMEM_LIMIT = 62 * 1024 * 1024


def _gelu(x):
    return 0.5 * x * (1.0 + lax.erf(x * SQRT_HALF))


def _rms(x, g):
    return x * lax.rsqrt(jnp.mean(x * x, axis=-1, keepdims=True) + EPS) * g


def _const_spec(shape, block_index=None):
    idx = (0,) * len(shape) if block_index is None else block_index
    return pl.BlockSpec(shape, lambda *_: idx, pipeline_mode=pl.Buffered(1))


def _bias_tiles_kernel(rpb_ref, o_ref):
    n_dr = 2 * WIN_ROWS - 1
    q_col = lax.broadcasted_iota(jnp.int32, (GRID_W, 128), 0)
    lane = lax.broadcasted_iota(jnp.int32, (GRID_W, 128), 1)
    k_col = lane % GRID_W
    win_start = jnp.clip(q_col - WIN_COLS // 2, 0, GRID_W - WIN_COLS)
    valid = (k_col >= win_start) & (k_col < win_start + WIN_COLS)
    low_half = lane < GRID_W

    def toeplitz(h, dr, lane_offset):
        if not 0 <= dr < n_dr:
            return jnp.zeros((GRID_W, 128), jnp.float32)
        row = jnp.broadcast_to(rpb_ref[h, dr:dr + 1, :], (GRID_W, 128))
        shift = (lane_offset - (WIN_COLS - 1)) % 128
        return pltpu.roll(row, shift, axis=1, stride=1, stride_axis=0)

    for copy in range(2):
        for h in range(B_HEADS):
            g, hh = divmod(h, HEADS_PER_GROUP)
            for t in range(WIN_ROWS):
                dr_lo = 2 * t - copy
                tile = jnp.where(low_half, toeplitz(h, dr_lo, 0), toeplitz(h, dr_lo + 1, GRID_W))
                o_ref[copy, g, t, hh * GRID_W:(hh + 1) * GRID_W, :] = jnp.where(
                    valid, tile * LOG2E, NEG_INF)


def _bias_tiles(rpb):
    rpb_pad = jnp.pad(rpb, ((0, 0), (0, 0), (0, 128 - rpb.shape[-1])))
    shape = (2, N_HEAD_GROUPS, WIN_ROWS, HEADS_PER_GROUP * GRID_W, 128)
    return pl.pallas_call(
        _bias_tiles_kernel,
        out_shape=jax.ShapeDtypeStruct(shape, jnp.float32),
        name="bias_tiles",
    )(rpb_pad)


def _mix_in_kernel(xa_ref, xb_ref, *refs, n_tiles_a):
    step = pl.program_id(0)
    pl.when(step < n_tiles_a)(functools.partial(_mix_in_tile, xa_ref, *refs))
    pl.when(step >= n_tiles_a)(functools.partial(_mix_in_tile, xb_ref, *refs))


def _mix_in_tile(x_ref, g1_ref, win_ref, gg_ref, ws_ref, bs_ref, ya_ref, q_ref, k_ref, v_ref):
    tm = x_ref.shape[0]
    h = _rms(x_ref[...], g1_ref[...]).astype(jnp.bfloat16)
    z = jnp.dot(h, win_ref[...], preferred_element_type=jnp.float32)
    u = _gelu(z[:, :A_WIDTH])
    va = _rms(_gelu(z[:, A_WIDTH:2 * A_WIDTH]), gg_ref[...]).astype(jnp.bfloat16)
    q_scale = HEAD_DIM ** -0.5 * LOG2E
    q_ref[...] = (z[:, 2 * A_WIDTH:2 * A_WIDTH + B_WIDTH] * q_scale).astype(jnp.bfloat16)
    k_ref[...] = z[:, 2 * A_WIDTH + B_WIDTH:2 * A_WIDTH + 2 * B_WIDTH].astype(jnp.bfloat16)
    v_ref[...] = z[:, 2 * A_WIDTH + 2 * B_WIDTH:].astype(jnp.bfloat16)

    first_half = lax.broadcasted_iota(jnp.int32, (CHUNK, 128), 1) < HEAD_DIM
    for c in range(tm // CHUNK):
        rows = slice(c * CHUNK, (c + 1) * CHUNK)
        for j in range(A_WIDTH // 128):
            cols = slice(j * 128, (j + 1) * 128)
            va_t = va[rows, cols]
            s0 = jnp.dot(ws_ref[2 * j], va_t, preferred_element_type=jnp.float32)
            s1 = jnp.dot(ws_ref[2 * j + 1], va_t, preferred_element_type=jnp.float32)
            s = jnp.where(first_half, s0, s1) + bs_ref[:, cols]
            ya_ref[rows, cols] = (u[rows, cols] * s).astype(jnp.bfloat16)


def _mix_in(xa, xb, p):
    tm = MIX_TM
    n_a, n_b = xa.shape[0] // tm, xb.shape[0] // tm
    n = xa.shape[0] + xb.shape[0]
    xa_spec = pl.BlockSpec((tm, D_MODEL), lambda i: (jnp.minimum(i, n_a - 1), 0))
    xb_spec = pl.BlockSpec((tm, D_MODEL), lambda i: (jnp.maximum(i - n_a, 0), 0))
    tok = pl.BlockSpec((tm, A_WIDTH), lambda i: (i, 0))
    out = jax.ShapeDtypeStruct((n, A_WIDTH), jnp.bfloat16)
    return pl.pallas_call(
        functools.partial(_mix_in_kernel, n_tiles_a=n_a),
        grid=(n_a + n_b,),
        in_specs=[xa_spec, xb_spec, _const_spec((1, D_MODEL)), _const_spec((D_MODEL, IN_WIDTH)),
                  _const_spec((1, A_WIDTH)), _const_spec((A_GROUPS, CHUNK, CHUNK)),
                  _const_spec((CHUNK, A_WIDTH))],
        out_specs=[tok] * 4,
        out_shape=[out] * 4,
        compiler_params=pltpu.CompilerParams(dimension_semantics=("arbitrary",),
                                             vmem_limit_bytes=VMEM_LIMIT),
        name="mix_in",
    )(xa, xb, p["g1"], p["w_in"], p["gate_g"], p["w_s"], p["b_s"])


def _att_step_info(step, segs):
    (batch_a, rows_a), (_, rows_b) = segs
    per_a, per_b = rows_a // ATT_ROWS, rows_b // ATT_ROWS
    steps_a = batch_a * per_a
    in_a = step < steps_a
    step_b = jnp.maximum(step - steps_a, 0)
    i = jnp.where(in_a, step % per_a, step_b % per_b)
    n_rows = jnp.where(in_a, rows_a, rows_b)
    seq_block = jnp.where(in_a, (step // per_a) * (rows_a // ATT_HALO_ROWS),
                          batch_a * (rows_a // ATT_HALO_ROWS)
                          + (step_b // per_b) * (rows_b // ATT_HALO_ROWS))
    win_block = jnp.clip(i * (ATT_ROWS // ATT_HALO_ROWS) - 1, 0,
                         (n_rows - ATT_WIN_ROWS) // ATT_HALO_ROWS)
    return i, n_rows, win_block, seq_block


def _natten_kernel(q_ref, k_ref, v_ref, bias_ref, o_ref, *, segs):
    i, n_rows, win_block, _ = _att_step_info(pl.program_id(0), segs)
    first_row = win_block * ATT_HALO_ROWS

    row_blk = lax.broadcasted_iota(jnp.int32, (GROUP_W, GROUP_W), 0) // HEAD_DIM
    lane_blk = lax.broadcasted_iota(jnp.int32, (GROUP_W, GROUP_W), 1) // HEAD_DIM
    diag = row_blk == lane_blk
    low_half = lax.broadcasted_iota(jnp.int32, (GRID_W, 128), 1) < HEAD_DIM

    def row_body(rl, carry):
        r = i * ATT_ROWS + rl
        rs = jnp.clip(r - WIN_ROWS // 2, 0, n_rows - WIN_ROWS)
        cls = r - rs
        bias_copy = (cls + 1) % 2
        bias_tile = (WIN_ROWS - cls) // 2
        koff = pl.multiple_of((rs - first_row) * GRID_W, GRID_W)
        qoff = pl.multiple_of(rl * GRID_W, GRID_W)
        for g in range(N_HEAD_GROUPS):
            lanes = slice(g * GROUP_W, (g + 1) * GROUP_W)
            q_g = q_ref[pl.ds(qoff, GRID_W), lanes]
            q_bd = jnp.where(diag, jnp.concatenate([q_g] * HEADS_PER_GROUP, axis=0),
                             jnp.zeros((), jnp.bfloat16))
            k_w = k_ref[pl.ds(koff, WIN_KEYS), lanes]
            s = lax.dot_general(q_bd, k_w, (((1,), (1,)), ((), ())),
                                preferred_element_type=jnp.float32)
            s = s + jnp.concatenate([bias_ref[bias_copy, g, bias_tile + jt]
                                     for jt in range(WIN_KEYS // 128)], axis=1)
            m = jnp.max(s, axis=-1, keepdims=True)
            e = jnp.exp2(s - m)
            inv_l = 1.0 / jnp.sum(e, axis=-1, keepdims=True)
            v_w = v_ref[pl.ds(koff, WIN_KEYS), lanes]
            o = jnp.dot(e.astype(jnp.bfloat16), v_w, preferred_element_type=jnp.float32)
            tiles = []
            for t2 in range(GROUP_W // 128):
                blk = []
                for hh in (2 * t2, 2 * t2 + 1):
                    rws = slice(hh * GRID_W, (hh + 1) * GRID_W)
                    blk.append(o[rws, t2 * 128:(t2 + 1) * 128] * inv_l[rws])
                tiles.append(jnp.where(low_half, blk[0], blk[1]))
            o_ref[pl.ds(qoff, GRID_W), lanes] = jnp.concatenate(tiles, axis=1).astype(jnp.bfloat16)
        return carry

    lax.fori_loop(0, ATT_ROWS, row_body, 0, unroll=True)


def _natten(q, k, v, bias, segs):
    n = q.shape[0]
    halo_tokens = ATT_HALO_ROWS * GRID_W
    q_spec = pl.BlockSpec((ATT_ROWS * GRID_W, B_WIDTH), lambda s: (s, 0))

    def kv_index(s):
        _, _, win_block, seq_block = _att_step_info(s, segs)
        return ((seq_block + win_block) * halo_tokens, 0)

    kv_spec = pl.BlockSpec((pl.Element(ATT_WIN_ROWS * GRID_W), pl.Element(B_WIDTH)), kv_index)
    return pl.pallas_call(
        functools.partial(_natten_kernel, segs=segs),
        grid=(n // (ATT_ROWS * GRID_W),),
        in_specs=[q_spec, kv_spec, kv_spec, _const_spec(bias.shape)],
        out_specs=q_spec,
        out_shape=jax.ShapeDtypeStruct((n, B_WIDTH), jnp.bfloat16),
        compiler_params=pltpu.CompilerParams(dimension_semantics=("arbitrary",),
                                             vmem_limit_bytes=VMEM_LIMIT),
        name="natten",
    )(q, k, v, bias)


def _ffn_out_kernel(xp_ref, x_ref, xn_ref, yap_ref, ya_ref, yan_ref, ybp_ref, yb_ref, ybn_ref,
                    woa_ref, wob_ref, g2_ref, wa_ref, wb_ref, cw_ref, cb_ref, wd_ref, gf_ref,
                    o_ref, *, tiles_per_seq):
    tm = x_ref.shape[0]
    j = pl.program_id(0) % tiles_per_seq
    has_prev = j > 0
    has_next = j < tiles_per_seq - 1
    lo, hi = HALO_F32, HALO_F32 + tm

    def ext(prev_ref, cur_ref, next_ref):
        prev = prev_ref[...].astype(jnp.float32)
        nxt = next_ref[...].astype(jnp.float32)
        prev = jnp.where(has_prev, prev[prev.shape[0] - HALO_F32:], 0.0)
        nxt = jnp.where(has_next, nxt[:HALO_F32], 0.0)
        return jnp.concatenate([prev, cur_ref[...].astype(jnp.float32), nxt], axis=0)

    x_e = ext(xp_ref, x_ref, xn_ref)
    ya_e = ext(yap_ref, ya_ref, yan_ref).astype(jnp.bfloat16)
    yb_e = ext(ybp_ref, yb_ref, ybn_ref).astype(jnp.bfloat16)
    x1 = (x_e + jnp.dot(ya_e, woa_ref[...], preferred_element_type=jnp.float32)
          + jnp.dot(yb_e, wob_ref[...], preferred_element_type=jnp.float32))
    h2 = _rms(x1, g2_ref[...])
    a_e = jnp.dot(h2.astype(jnp.bfloat16), wa_ref[...], preferred_element_type=jnp.float32)
    b = jnp.dot(h2[lo:hi].astype(jnp.bfloat16), wb_ref[...], preferred_element_type=jnp.float32)
    n_ext = tm + 2 * HALO_F32
    a_prev = pltpu.roll(a_e, 1, axis=0)[lo:hi]
    a_next = pltpu.roll(a_e, n_ext - 1, axis=0)[lo:hi]
    a_conv = (cw_ref[0:1, :] * a_prev + cw_ref[1:2, :] * a_e[lo:hi] + cw_ref[2:3, :] * a_next
              + cb_ref[...])
    gated = (_gelu(a_conv) * b).astype(jnp.bfloat16)
    x2 = x1[lo:hi] + jnp.dot(gated, wd_ref[...], preferred_element_type=jnp.float32)
    o_ref[...] = _rms(x2, gf_ref[...])


def _ffn_out(x2d, ya, yb, p, seq_len, token_offset):
    n = x2d.shape[0]
    tm = FFN_TM
    tiles_per_seq = seq_len // tm
    tile_offset = token_offset // tm

    def halo_specs(arr, rows, off):
        width = arr.shape[1]
        per = tm // rows
        last = arr.shape[0] // rows - 1
        prev = pl.BlockSpec((rows, width), lambda i: (jnp.maximum((i + off) * per - 1, 0), 0))
        cur = pl.BlockSpec((tm, width), lambda i: (i + off, 0))
        nxt = pl.BlockSpec((rows, width), lambda i: (jnp.minimum((i + off + 1) * per, last), 0))
        return [prev, cur, nxt]

    in_specs = (halo_specs(x2d, HALO_F32, 0) + halo_specs(ya, HALO_BF16, tile_offset)
                + halo_specs(yb, HALO_BF16, tile_offset)
                + [_const_spec((A_WIDTH, D_MODEL)), _const_spec((B_WIDTH, D_MODEL), (1, 0)),
                   _const_spec((1, D_MODEL)), _const_spec((D_MODEL, D_FF)),
                   _const_spec((D_MODEL, D_FF), (0, 1)), _const_spec((3, D_FF)),
                   _const_spec((1, D_FF)),
                   _const_spec((D_FF, D_MODEL)), _const_spec((1, D_MODEL))])
    return pl.pallas_call(
        functools.partial(_ffn_out_kernel, tiles_per_seq=tiles_per_seq),
        grid=(n // tm,),
        in_specs=in_specs,
        out_specs=pl.BlockSpec((tm, D_MODEL), lambda i: (i, 0)),
        out_shape=jax.ShapeDtypeStruct((n, D_MODEL), jnp.float32),
        compiler_params=pltpu.CompilerParams(dimension_semantics=("arbitrary",),
                                             vmem_limit_bytes=VMEM_LIMIT),
        name="ffn_out",
    )(x2d, x2d, x2d, ya, ya, ya, yb, yb, yb,
      p["w_out"], p["w_out"], p["g2"], p["w_up"], p["w_up"], p["conv_w"], p["conv_b"],
      p["w_down"], p["gf"])


def _prepare(norm1_g, w_in, gate_norm_g, w_spatial, b_spatial, rpb, w_out, norm2_g, w_up,
             conv_w, conv_b, w_down, final_norm_g):
    bf = jnp.bfloat16
    return {
        "g1": norm1_g[0][None, :],
        "w_in": w_in[0].astype(bf),
        "gate_g": gate_norm_g[0][None, :],
        "w_s": w_spatial[0].astype(bf),
        "b_s": jnp.repeat(b_spatial[0].T, HEAD_DIM, axis=1),
        "bias": _bias_tiles(rpb[0]),
        "w_out": w_out[0].astype(bf),
        "g2": norm2_g[0][None, :],
        "w_up": w_up[0].astype(bf),
        "conv_w": conv_w[0],
        "conv_b": conv_b[0][None, :],
        "w_down": w_down[0].astype(bf),
        "gf": final_norm_g[None, :],
    }


def kernel(x_prompt, x_sample, norm1_g, w_in, gate_norm_g, w_spatial, b_spatial, rpb, w_out,
           norm2_g, w_up, conv_w, conv_b, w_down, final_norm_g):
    p = _prepare(norm1_g, w_in, gate_norm_g, w_spatial, b_spatial, rpb, w_out, norm2_g, w_up,
                 conv_w, conv_b, w_down, final_norm_g)
    xs = [x.reshape(-1, D_MODEL) for x in (x_prompt, x_sample)]
    segs = tuple((x.shape[0], x.shape[1] // GRID_W) for x in (x_prompt, x_sample))
    ya, q, k, v = _mix_in(xs[0], xs[1], p)
    yb = _natten(q, k, v, p["bias"], segs)
    outs, offset = [], 0
    for x, x2d in zip((x_prompt, x_sample), xs):
        y = _ffn_out(x2d, ya, yb, p, x.shape[1], offset)
        outs.append(y.reshape(x.shape))
        offset += x2d.shape[0]
    return tuple(outs)
```

```python
import functools
import math

import jax
import jax.numpy as jnp
from jax import lax
from jax.experimental import pallas as pl
from jax.experimental.pallas import tpu as pltpu

D_MODEL = 1024
GRID_W = 64
A_WIDTH = 512
B_WIDTH = 512
HEAD_DIM = 64
A_GROUPS = 8
B_HEADS = 8
IN_WIDTH = 2 * A_WIDTH + 3 * B_WIDTH
CHUNK = 128
WIN_ROWS = 8
WIN_COLS = 16
D_FF = 2816
EPS = 1e-6
NEG_INF = -1e30
SQRT_HALF = math.sqrt(0.5)
LOG2E = math.log2(math.e)

HEADS_PER_GROUP = 4
GROUP_W = HEADS_PER_GROUP * HEAD_DIM
N_HEAD_GROUPS = B_HEADS // HEADS_PER_GROUP
WIN_KEYS = WIN_ROWS * GRID_W

MIX_TM = 1024
ATT_ROWS = 32
ATT_HALO_ROWS = 8
ATT_WIN_ROWS = ATT_ROWS + 2 * ATT_HALO_ROWS
FFN_TM = 1024
HALO_F32 = 8
HALO_BF16 = 16
VMEM_LIMIT = 62 * 1024 * 1024


def _gelu(x):
    return 0.5 * x * (1.0 + lax.erf(x * SQRT_HALF))


def _rms(x, g):
    return x * lax.rsqrt(jnp.mean(x * x, axis=-1, keepdims=True) + EPS) * g


def _const_spec(shape, block_index=None):
    idx = (0,) * len(shape) if block_index is None else block_index
    return pl.BlockSpec(shape, lambda *_: idx, pipeline_mode=pl.Buffered(1))


def _bias_tiles_kernel(rpb_ref, o_ref):
    n_dr = 2 * WIN_ROWS - 1
    q_col = lax.broadcasted_iota(jnp.int32, (GRID_W, 128), 0)
    lane = lax.broadcasted_iota(jnp.int32, (GRID_W, 128), 1)
    k_col = lane % GRID_W
    win_start = jnp.clip(q_col - WIN_COLS // 2, 0, GRID_W - WIN_COLS)
    valid = (k_col >= win_start) & (k_col < win_start + WIN_COLS)
    low_half = lane < GRID_W

    def toeplitz(h, dr, lane_offset):
        if not 0 <= dr < n_dr:
            return jnp.zeros((GRID_W, 128), jnp.float32)
        row = jnp.broadcast_to(rpb_ref[h, dr:dr + 1, :], (GRID_W, 128))
        shift = (lane_offset - (WIN_COLS - 1)) % 128
        return pltpu.roll(row, shift, axis=1, stride=1, stride_axis=0)

    for copy in range(2):
        for h in range(B_HEADS):
            g, hh = divmod(h, HEADS_PER_GROUP)
            for t in range(WIN_ROWS):
                dr_lo = 2 * t - copy
                tile = jnp.where(low_half, toeplitz(h, dr_lo, 0), toeplitz(h, dr_lo + 1, GRID_W))
                o_ref[copy, g, t, hh * GRID_W:(hh + 1) * GRID_W, :] = jnp.where(
                    valid, tile * LOG2E, NEG_INF)


def _bias_tiles(rpb):
    rpb_pad = jnp.pad(rpb, ((0, 0), (0, 0), (0, 128 - rpb.shape[-1])))
    shape = (2, N_HEAD_GROUPS, WIN_ROWS, HEADS_PER_GROUP * GRID_W, 128)
    return pl.pallas_call(
        _bias_tiles_kernel,
        out_shape=jax.ShapeDtypeStruct(shape, jnp.float32),
        name="bias_tiles",
    )(rpb_pad)


def _mix_in_kernel(xa_ref, xb_ref, g1_ref, win32_ref, gg_ref, ws32_ref, bs_ref, *rest, n_tiles_a):
    n_cast = len(_FFN_WEIGHT_SLICES)
    cast_src, outs, cast_dst = rest[:n_cast], rest[n_cast:n_cast + 4], rest[n_cast + 4:2 * n_cast + 4]
    win_ref, ws_ref = rest[2 * n_cast + 4:]
    step = pl.program_id(0)

    @pl.when(step == 0)
    def _():
        win_ref[...] = win32_ref[...].astype(jnp.bfloat16)
        ws_ref[...] = ws32_ref[...].astype(jnp.bfloat16)

    tile = functools.partial(_mix_in_tile, g1_ref=g1_ref, win_ref=win_ref, gg_ref=gg_ref,
                             ws_ref=ws_ref, bs_ref=bs_ref, outs=outs, cast=(cast_src, cast_dst))
    pl.when(step < n_tiles_a)(functools.partial(tile, xa_ref))
    pl.when(step >= n_tiles_a)(functools.partial(tile, xb_ref))


def _mix_in_tile(x_ref, *, g1_ref, win_ref, gg_ref, ws_ref, bs_ref, outs, cast):
    ya_ref, q_ref, k_ref, v_ref = outs
    for src, dst in zip(*cast):
        dst[...] = src[...].astype(jnp.bfloat16)
    tm = x_ref.shape[0]
    h = _rms(x_ref[...], g1_ref[...]).astype(jnp.bfloat16)
    z = jnp.dot(h, win_ref[...], preferred_element_type=jnp.float32)
    u = _gelu(z[:, :A_WIDTH])
    va = _rms(_gelu(z[:, A_WIDTH:2 * A_WIDTH]), gg_ref[...]).astype(jnp.bfloat16)
    q_scale = HEAD_DIM ** -0.5 * LOG2E
    q_ref[...] = (z[:, 2 * A_WIDTH:2 * A_WIDTH + B_WIDTH] * q_scale).astype(jnp.bfloat16)
    k_ref[...] = z[:, 2 * A_WIDTH + B_WIDTH:2 * A_WIDTH + 2 * B_WIDTH].astype(jnp.bfloat16)
    v_ref[...] = z[:, 2 * A_WIDTH + 2 * B_WIDTH:].astype(jnp.bfloat16)

    first_half = lax.broadcasted_iota(jnp.int32, (CHUNK, 128), 1) < HEAD_DIM
    for c in range(tm // CHUNK):
        rows = slice(c * CHUNK, (c + 1) * CHUNK)
        for j in range(A_WIDTH // 128):
            cols = slice(j * 128, (j + 1) * 128)
            va_t = va[rows, cols]
            s0 = jnp.dot(ws_ref[2 * j], va_t, preferred_element_type=jnp.float32)
            s1 = jnp.dot(ws_ref[2 * j + 1], va_t, preferred_element_type=jnp.float32)
            s = jnp.where(first_half, s0, s1) + bs_ref[:, cols]
            ya_ref[rows, cols] = (u[rows, cols] * s).astype(jnp.bfloat16)


_FFN_WEIGHT_SLICES = ("w_out", "w_up", "w_down")


def _rows_per_step(rows, n_steps):
    per = HALO_BF16
    while rows % per or rows // per > n_steps:
        per += HALO_BF16
    return per


def _mix_in(xa, xb, p):
    tm = MIX_TM
    n_a, n_b = xa.shape[0] // tm, xb.shape[0] // tm
    n = xa.shape[0] + xb.shape[0]
    xa_spec = pl.BlockSpec((tm, D_MODEL), lambda i: (jnp.minimum(i, n_a - 1), 0))
    xb_spec = pl.BlockSpec((tm, D_MODEL), lambda i: (jnp.maximum(i - n_a, 0), 0))
    tok = pl.BlockSpec((tm, A_WIDTH), lambda i: (i, 0))
    out = jax.ShapeDtypeStruct((n, A_WIDTH), jnp.bfloat16)

    weights = [p[name] for name in _FFN_WEIGHT_SLICES]
    cast_specs = []
    for w in weights:
        per = _rows_per_step(w.shape[0], n_a + n_b)
        last = w.shape[0] // per - 1
        cast_specs.append(pl.BlockSpec((per, w.shape[1]),
                                       lambda i, last=last: (jnp.minimum(i, last), 0)))
    cast_shapes = [jax.ShapeDtypeStruct(w.shape, jnp.bfloat16) for w in weights]

    res = pl.pallas_call(
        functools.partial(_mix_in_kernel, n_tiles_a=n_a),
        grid=(n_a + n_b,),
        in_specs=[xa_spec, xb_spec, _const_spec((1, D_MODEL)), _const_spec((D_MODEL, IN_WIDTH)),
                  _const_spec((1, A_WIDTH)), _const_spec((A_GROUPS, CHUNK, CHUNK)),
                  _const_spec((CHUNK, A_WIDTH))] + cast_specs,
        out_specs=[tok] * 4 + cast_specs,
        out_shape=[out] * 4 + cast_shapes,
        scratch_shapes=[pltpu.VMEM((D_MODEL, IN_WIDTH), jnp.bfloat16),
                        pltpu.VMEM((A_GROUPS, CHUNK, CHUNK), jnp.bfloat16)],
        compiler_params=pltpu.CompilerParams(dimension_semantics=("arbitrary",),
                                             vmem_limit_bytes=VMEM_LIMIT),
        name="mix_in",
    )(xa, xb, p["g1"], p["w_in"], p["gate_g"], p["w_s"], p["b_s"], *weights)
    return res[:4], dict(zip(_FFN_WEIGHT_SLICES, res[4:]))


def _att_step_info(step, segs):
    (batch_a, rows_a), (_, rows_b) = segs
    per_a, per_b = rows_a // ATT_ROWS, rows_b // ATT_ROWS
    steps_a = batch_a * per_a
    in_a = step < steps_a
    step_b = jnp.maximum(step - steps_a, 0)
    i = jnp.where(in_a, step % per_a, step_b % per_b)
    n_rows = jnp.where(in_a, rows_a, rows_b)
    seq_block = jnp.where(in_a, (step // per_a) * (rows_a // ATT_HALO_ROWS),
                          batch_a * (rows_a // ATT_HALO_ROWS)
                          + (step_b // per_b) * (rows_b // ATT_HALO_ROWS))
    win_block = jnp.clip(i * (ATT_ROWS // ATT_HALO_ROWS) - 1, 0,
                         (n_rows - ATT_WIN_ROWS) // ATT_HALO_ROWS)
    return i, n_rows, win_block, seq_block


def _natten_kernel(q_ref, k_ref, v_ref, bias_ref, o_ref, *, segs):
    i, n_rows, win_block, _ = _att_step_info(pl.program_id(0), segs)
    first_row = win_block * ATT_HALO_ROWS

    row_blk = lax.broadcasted_iota(jnp.int32, (GROUP_W, GROUP_W), 0) // HEAD_DIM
    lane_blk = lax.broadcasted_iota(jnp.int32, (GROUP_W, GROUP_W), 1) // HEAD_DIM
    diag = row_blk == lane_blk
    low_half = lax.broadcasted_iota(jnp.int32, (GRID_W, 128), 1) < HEAD_DIM

    def row_body(rl, carry):
        r = i * ATT_ROWS + rl
        rs = jnp.clip(r - WIN_ROWS // 2, 0, n_rows - WIN_ROWS)
        cls = r - rs
        bias_copy = (cls + 1) % 2
        bias_tile = (WIN_ROWS - cls) // 2
        koff = pl.multiple_of((rs - first_row) * GRID_W, GRID_W)
        qoff = pl.multiple_of(rl * GRID_W, GRID_W)
        for g in range(N_HEAD_GROUPS):
            lanes = slice(g * GROUP_W, (g + 1) * GROUP_W)
            q_g = q_ref[pl.ds(qoff, GRID_W), lanes]
            q_bd = jnp.where(diag, jnp.concatenate([q_g] * HEADS_PER_GROUP, axis=0),
                             jnp.zeros((), jnp.bfloat16))
            k_w = k_ref[pl.ds(koff, WIN_KEYS), lanes]
            s = lax.dot_general(q_bd, k_w, (((1,), (1,)), ((), ())),
                                preferred_element_type=jnp.float32)
            s = s + jnp.concatenate([bias_ref[bias_copy, g, bias_tile + jt]
                                     for jt in range(WIN_KEYS // 128)], axis=1)
            m = jnp.max(s, axis=-1, keepdims=True)
            e = jnp.exp2(s - m)
            inv_l = 1.0 / jnp.sum(e, axis=-1, keepdims=True)
            v_w = v_ref[pl.ds(koff, WIN_KEYS), lanes]
            o = jnp.dot(e.astype(jnp.bfloat16), v_w, preferred_element_type=jnp.float32)
            tiles = []
            for t2 in range(GROUP_W // 128):
                blk = []
                for hh in (2 * t2, 2 * t2 + 1):
                    rws = slice(hh * GRID_W, (hh + 1) * GRID_W)
                    blk.append(o[rws, t2 * 128:(t2 + 1) * 128] * inv_l[rws])
                tiles.append(jnp.where(low_half, blk[0], blk[1]))
            o_ref[pl.ds(qoff, GRID_W), lanes] = jnp.concatenate(tiles, axis=1).astype(jnp.bfloat16)
        return carry

    lax.fori_loop(0, ATT_ROWS, row_body, 0, unroll=True)


def _natten(q, k, v, bias, segs):
    n = q.shape[0]
    halo_tokens = ATT_HALO_ROWS * GRID_W
    q_spec = pl.BlockSpec((ATT_ROWS * GRID_W, B_WIDTH), lambda s: (s, 0))

    def kv_index(s):
        _, _, win_block, seq_block = _att_step_info(s, segs)
        return ((seq_block + win_block) * halo_tokens, 0)

    kv_spec = pl.BlockSpec((pl.Element(ATT_WIN_ROWS * GRID_W), pl.Element(B_WIDTH)), kv_index)
    return pl.pallas_call(
        functools.partial(_natten_kernel, segs=segs),
        grid=(n // (ATT_ROWS * GRID_W),),
        in_specs=[q_spec, kv_spec, kv_spec, _const_spec(bias.shape)],
        out_specs=q_spec,
        out_shape=jax.ShapeDtypeStruct((n, B_WIDTH), jnp.bfloat16),
        compiler_params=pltpu.CompilerParams(dimension_semantics=("arbitrary",),
                                             vmem_limit_bytes=VMEM_LIMIT),
        name="natten",
    )(q, k, v, bias)


def _ffn_out_kernel(xp_ref, x_ref, xn_ref, yap_ref, ya_ref, yan_ref, ybp_ref, yb_ref, ybn_ref,
                    woa_ref, wob_ref, g2_ref, wa_ref, wb_ref, cw_ref, cb_ref, wd_ref, gf_ref,
                    o_ref, *, tiles_per_seq):
    tm = x_ref.shape[0]
    j = pl.program_id(0) % tiles_per_seq
    has_prev = j > 0
    has_next = j < tiles_per_seq - 1
    lo, hi = HALO_F32, HALO_F32 + tm

    def ext(prev_ref, cur_ref, next_ref):
        prev = prev_ref[...].astype(jnp.float32)
        nxt = next_ref[...].astype(jnp.float32)
        prev = jnp.where(has_prev, prev[prev.shape[0] - HALO_F32:], 0.0)
        nxt = jnp.where(has_next, nxt[:HALO_F32], 0.0)
        return jnp.concatenate([prev, cur_ref[...].astype(jnp.float32), nxt], axis=0)

    x_e = ext(xp_ref, x_ref, xn_ref)
    ya_e = ext(yap_ref, ya_ref, yan_ref).astype(jnp.bfloat16)
    yb_e = ext(ybp_ref, yb_ref, ybn_ref).astype(jnp.bfloat16)
    x1 = (x_e + jnp.dot(ya_e, woa_ref[...], preferred_element_type=jnp.float32)
          + jnp.dot(yb_e, wob_ref[...], preferred_element_type=jnp.float32))
    h2 = _rms(x1, g2_ref[...])
    a_e = jnp.dot(h2.astype(jnp.bfloat16), wa_ref[...], preferred_element_type=jnp.float32)
    b = jnp.dot(h2[lo:hi].astype(jnp.bfloat16), wb_ref[...], preferred_element_type=jnp.float32)
    n_ext = tm + 2 * HALO_F32
    a_prev = pltpu.roll(a_e, 1, axis=0)[lo:hi]
    a_next = pltpu.roll(a_e, n_ext - 1, axis=0)[lo:hi]
    a_conv = (cw_ref[0:1, :] * a_prev + cw_ref[1:2, :] * a_e[lo:hi] + cw_ref[2:3, :] * a_next
              + cb_ref[...])
    gated = (_gelu(a_conv) * b).astype(jnp.bfloat16)
    x2 = x1[lo:hi] + jnp.dot(gated, wd_ref[...], preferred_element_type=jnp.float32)
    o_ref[...] = _rms(x2, gf_ref[...])


def _ffn_out(x2d, ya, yb, p, seq_len, token_offset):
    n = x2d.shape[0]
    tm = FFN_TM
    tiles_per_seq = seq_len // tm
    tile_offset = token_offset // tm

    def halo_specs(arr, rows, off):
        width = arr.shape[1]
        per = tm // rows
        last = arr.shape[0] // rows - 1
        prev = pl.BlockSpec((rows, width), lambda i: (jnp.maximum((i + off) * per - 1, 0), 0))
        cur = pl.BlockSpec((tm, width), lambda i: (i + off, 0))
        nxt = pl.BlockSpec((rows, width), lambda i: (jnp.minimum((i + off + 1) * per, last), 0))
        return [prev, cur, nxt]

    in_specs = (halo_specs(x2d, HALO_F32, 0) + halo_specs(ya, HALO_BF16, tile_offset)
                + halo_specs(yb, HALO_BF16, tile_offset)
                + [_const_spec((A_WIDTH, D_MODEL)), _const_spec((B_WIDTH, D_MODEL), (1, 0)),
                   _const_spec((1, D_MODEL)), _const_spec((D_MODEL, D_FF)),
                   _const_spec((D_MODEL, D_FF), (0, 1)), _const_spec((3, D_FF)),
                   _const_spec((1, D_FF)),
                   _const_spec((D_FF, D_MODEL)), _const_spec((1, D_MODEL))])
    return pl.pallas_call(
        functools.partial(_ffn_out_kernel, tiles_per_seq=tiles_per_seq),
        grid=(n // tm,),
        in_specs=in_specs,
        out_specs=pl.BlockSpec((tm, D_MODEL), lambda i: (i, 0)),
        out_shape=jax.ShapeDtypeStruct((n, D_MODEL), jnp.float32),
        compiler_params=pltpu.CompilerParams(dimension_semantics=("arbitrary",),
                                             vmem_limit_bytes=VMEM_LIMIT),
        name="ffn_out",
    )(x2d, x2d, x2d, ya, ya, ya, yb, yb, yb,
      p["w_out"], p["w_out"], p["g2"], p["w_up"], p["w_up"], p["conv_w"], p["conv_b"],
      p["w_down"], p["gf"])


def _prepare(norm1_g, w_in, gate_norm_g, w_spatial, b_spatial, rpb, w_out, norm2_g, w_up,
             conv_w, conv_b, w_down, final_norm_g):
    return {
        "g1": norm1_g[0][None, :],
        "w_in": w_in[0],
        "gate_g": gate_norm_g[0][None, :],
        "w_s": w_spatial[0],
        "b_s": jnp.repeat(b_spatial[0].T, HEAD_DIM, axis=1),
        "bias": _bias_tiles(rpb[0]),
        "w_out": w_out[0],
        "g2": norm2_g[0][None, :],
        "w_up": w_up[0],
        "conv_w": conv_w[0],
        "conv_b": conv_b[0][None, :],
        "w_down": w_down[0],
        "gf": final_norm_g[None, :],
    }


def kernel(x_prompt, x_sample, norm1_g, w_in, gate_norm_g, w_spatial, b_spatial, rpb, w_out,
           norm2_g, w_up, conv_w, conv_b, w_down, final_norm_g):
    p = _prepare(norm1_g, w_in, gate_norm_g, w_spatial, b_spatial, rpb, w_out, norm2_g, w_up,
                 conv_w, conv_b, w_down, final_norm_g)
    xs = [x.reshape(-1, D_MODEL) for x in (x_prompt, x_sample)]
    segs = tuple((x.shape[0], x.shape[1] // GRID_W) for x in (x_prompt, x_sample))
    (ya, q, k, v), ffn_weights = _mix_in(xs[0], xs[1], p)
    p = {**p, **ffn_weights}
    yb = _natten(q, k, v, p["bias"], segs)
    outs, offset = [], 0
    for x, x2d in zip((x_prompt, x_sample), xs):
        y = _ffn_out(x2d, ya, yb, p, x.shape[1], offset)
        outs.append(y.reshape(x.shape))
        offset += x2d.shape[0]
    return tuple(outs)
```

```python
import functools
import math

import jax
import jax.numpy as jnp
from jax import lax
from jax.experimental import pallas as pl
from jax.experimental.pallas import tpu as pltpu

D_MODEL = 1024
GRID_W = 64
A_WIDTH = 512
B_WIDTH = 512
HEAD_DIM = 64
A_GROUPS = 8
B_HEADS = 8
IN_WIDTH = 2 * A_WIDTH + 3 * B_WIDTH
CHUNK = 128
WIN_ROWS = 8
WIN_COLS = 16
D_FF = 2816
EPS = 1e-6
NEG_INF = -1e30
SQRT_HALF = math.sqrt(0.5)
LOG2E = math.log2(math.e)

HEADS_PER_GROUP = 4
GROUP_W = HEADS_PER_GROUP * HEAD_DIM
N_HEAD_GROUPS = B_HEADS // HEADS_PER_GROUP
WIN_KEYS = WIN_ROWS * GRID_W

MIX_TM = 1024
MIX_ROW_BLOCKS = 4
ATT_ROWS = 32
ATT_HALO_ROWS = 8
ATT_WIN_ROWS = ATT_ROWS + 2 * ATT_HALO_ROWS
FFN_TM = 1024
FFN_ROW_BLOCKS = 4
HALO_F32 = 8
HALO_BF16 = 16
VMEM_LIMIT = 62 * 1024 * 1024


def _gelu(x):
    return 0.5 * x * (1.0 + lax.erf(x * SQRT_HALF))


def _rms(x, g):
    return x * lax.rsqrt(jnp.mean(x * x, axis=-1, keepdims=True) + EPS) * g


def _const_spec(shape, block_index=None):
    idx = (0,) * len(shape) if block_index is None else block_index
    return pl.BlockSpec(shape, lambda *_: idx, pipeline_mode=pl.Buffered(1))


def _bias_tiles_kernel(rpb_ref, o_ref):
    n_dr = 2 * WIN_ROWS - 1
    q_col = lax.broadcasted_iota(jnp.int32, (GRID_W, 128), 0)
    lane = lax.broadcasted_iota(jnp.int32, (GRID_W, 128), 1)
    k_col = lane % GRID_W
    win_start = jnp.clip(q_col - WIN_COLS // 2, 0, GRID_W - WIN_COLS)
    valid = (k_col >= win_start) & (k_col < win_start + WIN_COLS)
    low_half = lane < GRID_W

    def toeplitz(h, dr, lane_offset):
        if not 0 <= dr < n_dr:
            return jnp.zeros((GRID_W, 128), jnp.float32)
        row = jnp.broadcast_to(rpb_ref[h, dr:dr + 1, :], (GRID_W, 128))
        shift = (lane_offset - (WIN_COLS - 1)) % 128
        return pltpu.roll(row, shift, axis=1, stride=1, stride_axis=0)

    for copy in range(2):
        for h in range(B_HEADS):
            g, hh = divmod(h, HEADS_PER_GROUP)
            for t in range(WIN_ROWS):
                dr_lo = 2 * t - copy
                tile = jnp.where(low_half, toeplitz(h, dr_lo, 0), toeplitz(h, dr_lo + 1, GRID_W))
                o_ref[copy, g, t, hh * GRID_W:(hh + 1) * GRID_W, :] = jnp.where(
                    valid, tile * LOG2E, NEG_INF)


def _bias_tiles(rpb):
    rpb_pad = jnp.pad(rpb, ((0, 0), (0, 0), (0, 128 - rpb.shape[-1])))
    shape = (2, N_HEAD_GROUPS, WIN_ROWS, HEADS_PER_GROUP * GRID_W, 128)
    return pl.pallas_call(
        _bias_tiles_kernel,
        out_shape=jax.ShapeDtypeStruct(shape, jnp.float32),
        name="bias_tiles",
    )(rpb_pad)


def _mix_in_kernel(xa_ref, xb_ref, g1_ref, win32_ref, gg_ref, ws32_ref, bs_ref, *rest, n_tiles_a):
    n_cast = len(_FFN_WEIGHT_SLICES)
    cast_src, outs, cast_dst = rest[:n_cast], rest[n_cast:n_cast + 4], rest[n_cast + 4:2 * n_cast + 4]
    win_ref, ws_ref = rest[2 * n_cast + 4:]
    step = pl.program_id(0)

    @pl.when(step == 0)
    def _():
        win_ref[...] = win32_ref[...].astype(jnp.bfloat16)
        ws_ref[...] = ws32_ref[...].astype(jnp.bfloat16)

    tile = functools.partial(_mix_in_tile, g1_ref=g1_ref, win_ref=win_ref, gg_ref=gg_ref,
                             ws_ref=ws_ref, bs_ref=bs_ref, outs=outs, cast=(cast_src, cast_dst))
    pl.when(step < n_tiles_a)(functools.partial(tile, xa_ref))
    pl.when(step >= n_tiles_a)(functools.partial(tile, xb_ref))


def _mix_in_tile(x_ref, *, g1_ref, win_ref, gg_ref, ws_ref, bs_ref, outs, cast):
    ya_ref, q_ref, k_ref, v_ref = outs
    for src, dst in zip(*cast):
        dst[...] = src[...].astype(jnp.bfloat16)
    tm = x_ref.shape[0]
    h = _rms(x_ref[...], g1_ref[...]).astype(jnp.bfloat16)
    blk = tm // MIX_ROW_BLOCKS
    z = jnp.concatenate([jnp.dot(h[r:r + blk], win_ref[...], preferred_element_type=jnp.float32)
                         for r in range(0, tm, blk)], axis=0)
    u = _gelu(z[:, :A_WIDTH])
    va = _rms(_gelu(z[:, A_WIDTH:2 * A_WIDTH]), gg_ref[...]).astype(jnp.bfloat16)
    q_scale = HEAD_DIM ** -0.5 * LOG2E
    q_ref[...] = (z[:, 2 * A_WIDTH:2 * A_WIDTH + B_WIDTH] * q_scale).astype(jnp.bfloat16)
    k_ref[...] = z[:, 2 * A_WIDTH + B_WIDTH:2 * A_WIDTH + 2 * B_WIDTH].astype(jnp.bfloat16)
    v_ref[...] = z[:, 2 * A_WIDTH + 2 * B_WIDTH:].astype(jnp.bfloat16)

    first_half = lax.broadcasted_iota(jnp.int32, (CHUNK, 128), 1) < HEAD_DIM
    for c in range(tm // CHUNK):
        rows = slice(c * CHUNK, (c + 1) * CHUNK)
        for j in range(A_WIDTH // 128):
            cols = slice(j * 128, (j + 1) * 128)
            va_t = va[rows, cols]
            s0 = jnp.dot(ws_ref[2 * j], va_t, preferred_element_type=jnp.float32)
            s1 = jnp.dot(ws_ref[2 * j + 1], va_t, preferred_element_type=jnp.float32)
            s = jnp.where(first_half, s0, s1) + bs_ref[:, cols]
            ya_ref[rows, cols] = (u[rows, cols] * s).astype(jnp.bfloat16)


_FFN_WEIGHT_SLICES = ("w_out", "w_up", "w_down")


def _rows_per_step(rows, n_steps):
    per = HALO_BF16
    while rows % per or rows // per > n_steps:
        per += HALO_BF16
    return per


def _mix_in(xa, xb, p):
    tm = MIX_TM
    n_a, n_b = xa.shape[0] // tm, xb.shape[0] // tm
    n = xa.shape[0] + xb.shape[0]
    xa_spec = pl.BlockSpec((tm, D_MODEL), lambda i: (jnp.minimum(i, n_a - 1), 0))
    xb_spec = pl.BlockSpec((tm, D_MODEL), lambda i: (jnp.maximum(i - n_a, 0), 0))
    tok = pl.BlockSpec((tm, A_WIDTH), lambda i: (i, 0))
    out = jax.ShapeDtypeStruct((n, A_WIDTH), jnp.bfloat16)

    weights = [p[name] for name in _FFN_WEIGHT_SLICES]
    cast_specs = []
    for w in weights:
        per = _rows_per_step(w.shape[0], n_a + n_b)
        last = w.shape[0] // per - 1
        cast_specs.append(pl.BlockSpec((per, w.shape[1]),
                                       lambda i, last=last: (jnp.minimum(i, last), 0)))
    cast_shapes = [jax.ShapeDtypeStruct(w.shape, jnp.bfloat16) for w in weights]

    res = pl.pallas_call(
        functools.partial(_mix_in_kernel, n_tiles_a=n_a),
        grid=(n_a + n_b,),
        in_specs=[xa_spec, xb_spec, _const_spec((1, D_MODEL)), _const_spec((D_MODEL, IN_WIDTH)),
                  _const_spec((1, A_WIDTH)), _const_spec((A_GROUPS, CHUNK, CHUNK)),
                  _const_spec((CHUNK, A_WIDTH))] + cast_specs,
        out_specs=[tok] * 4 + cast_specs,
        out_shape=[out] * 4 + cast_shapes,
        scratch_shapes=[pltpu.VMEM((D_MODEL, IN_WIDTH), jnp.bfloat16),
                        pltpu.VMEM((A_GROUPS, CHUNK, CHUNK), jnp.bfloat16)],
        compiler_params=pltpu.CompilerParams(dimension_semantics=("arbitrary",),
                                             vmem_limit_bytes=VMEM_LIMIT),
        name="mix_in",
    )(xa, xb, p["g1"], p["w_in"], p["gate_g"], p["w_s"], p["b_s"], *weights)
    return res[:4], dict(zip(_FFN_WEIGHT_SLICES, res[4:]))


def _att_step_info(step, segs):
    (batch_a, rows_a), (_, rows_b) = segs
    per_a, per_b = rows_a // ATT_ROWS, rows_b // ATT_ROWS
    steps_a = batch_a * per_a
    in_a = step < steps_a
    step_b = jnp.maximum(step - steps_a, 0)
    i = jnp.where(in_a, step % per_a, step_b % per_b)
    n_rows = jnp.where(in_a, rows_a, rows_b)
    seq_block = jnp.where(in_a, (step // per_a) * (rows_a // ATT_HALO_ROWS),
                          batch_a * (rows_a // ATT_HALO_ROWS)
                          + (step_b // per_b) * (rows_b // ATT_HALO_ROWS))
    win_block = jnp.clip(i * (ATT_ROWS // ATT_HALO_ROWS) - 1, 0,
                         (n_rows - ATT_WIN_ROWS) // ATT_HALO_ROWS)
    return i, n_rows, win_block, seq_block


def _natten_kernel(q_ref, k_ref, v_ref, bias_ref, o_ref, *, segs):
    i, n_rows, win_block, _ = _att_step_info(pl.program_id(0), segs)
    first_row = win_block * ATT_HALO_ROWS

    row_blk = lax.broadcasted_iota(jnp.int32, (GROUP_W, GROUP_W), 0) // HEAD_DIM
    lane_blk = lax.broadcasted_iota(jnp.int32, (GROUP_W, GROUP_W), 1) // HEAD_DIM
    diag = row_blk == lane_blk
    low_half = lax.broadcasted_iota(jnp.int32, (GRID_W, 128), 1) < HEAD_DIM

    def row_body(rl, carry):
        r = i * ATT_ROWS + rl
        rs = jnp.clip(r - WIN_ROWS // 2, 0, n_rows - WIN_ROWS)
        cls = r - rs
        bias_copy = (cls + 1) % 2
        bias_tile = (WIN_ROWS - cls) // 2
        koff = pl.multiple_of((rs - first_row) * GRID_W, GRID_W)
        qoff = pl.multiple_of(rl * GRID_W, GRID_W)
        for g in range(N_HEAD_GROUPS):
            lanes = slice(g * GROUP_W, (g + 1) * GROUP_W)
            q_g = q_ref[pl.ds(qoff, GRID_W), lanes]
            q_bd = jnp.where(diag, jnp.concatenate([q_g] * HEADS_PER_GROUP, axis=0),
                             jnp.zeros((), jnp.bfloat16))
            k_w = k_ref[pl.ds(koff, WIN_KEYS), lanes]
            s = lax.dot_general(q_bd, k_w, (((1,), (1,)), ((), ())),
                                preferred_element_type=jnp.float32)
            s = s + jnp.concatenate([bias_ref[bias_copy, g, bias_tile + jt]
                                     for jt in range(WIN_KEYS // 128)], axis=1)
            m = jnp.max(s, axis=-1, keepdims=True)
            e = jnp.exp2(s - m)
            inv_l = 1.0 / jnp.sum(e, axis=-1, keepdims=True)
            v_w = v_ref[pl.ds(koff, WIN_KEYS), lanes]
            o = jnp.dot(e.astype(jnp.bfloat16), v_w, preferred_element_type=jnp.float32)
            tiles = []
            for t2 in range(GROUP_W // 128):
                blk = []
                for hh in (2 * t2, 2 * t2 + 1):
                    rws = slice(hh * GRID_W, (hh + 1) * GRID_W)
                    blk.append(o[rws, t2 * 128:(t2 + 1) * 128] * inv_l[rws])
                tiles.append(jnp.where(low_half, blk[0], blk[1]))
            o_ref[pl.ds(qoff, GRID_W), lanes] = jnp.concatenate(tiles, axis=1).astype(jnp.bfloat16)
        return carry

    lax.fori_loop(0, ATT_ROWS, row_body, 0, unroll=True)


def _natten(q, k, v, bias, segs):
    n = q.shape[0]
    halo_tokens = ATT_HALO_ROWS * GRID_W
    q_spec = pl.BlockSpec((ATT_ROWS * GRID_W, B_WIDTH), lambda s: (s, 0))

    def kv_index(s):
        _, _, win_block, seq_block = _att_step_info(s, segs)
        return ((seq_block + win_block) * halo_tokens, 0)

    kv_spec = pl.BlockSpec((pl.Element(ATT_WIN_ROWS * GRID_W), pl.Element(B_WIDTH)), kv_index)
    return pl.pallas_call(
        functools.partial(_natten_kernel, segs=segs),
        grid=(n // (ATT_ROWS * GRID_W),),
        in_specs=[q_spec, kv_spec, kv_spec, _const_spec(bias.shape)],
        out_specs=q_spec,
        out_shape=jax.ShapeDtypeStruct((n, B_WIDTH), jnp.bfloat16),
        compiler_params=pltpu.CompilerParams(dimension_semantics=("arbitrary",),
                                             vmem_limit_bytes=VMEM_LIMIT),
        name="natten",
    )(q, k, v, bias)


def _ffn_out_kernel(xp_ref, x_ref, xn_ref, yap_ref, ya_ref, yan_ref, ybp_ref, yb_ref, ybn_ref,
                    woa_ref, wob_ref, g2_ref, wa_ref, wb_ref, cw_ref, cb_ref, wd_ref, gf_ref,
                    o_ref, *, tiles_per_seq):
    tm = x_ref.shape[0]
    j = pl.program_id(0) % tiles_per_seq
    has_prev = j > 0
    has_next = j < tiles_per_seq - 1
    lo, hi = HALO_F32, HALO_F32 + tm

    def ext(prev_ref, cur_ref, next_ref):
        prev = prev_ref[...].astype(jnp.float32)
        nxt = next_ref[...].astype(jnp.float32)
        prev = jnp.where(has_prev, prev[prev.shape[0] - HALO_F32:], 0.0)
        nxt = jnp.where(has_next, nxt[:HALO_F32], 0.0)
        return jnp.concatenate([prev, cur_ref[...].astype(jnp.float32), nxt], axis=0)

    x_e = ext(xp_ref, x_ref, xn_ref)
    ya_e = ext(yap_ref, ya_ref, yan_ref).astype(jnp.bfloat16)
    yb_e = ext(ybp_ref, yb_ref, ybn_ref).astype(jnp.bfloat16)
    n_ext = tm + 2 * HALO_F32
    blk = tm // FFN_ROW_BLOCKS
    cuts = [0] + [n_ext - blk * k for k in range(FFN_ROW_BLOCKS - 1, -1, -1)]
    x1 = jnp.concatenate(
        [x_e[s:e] + jnp.dot(ya_e[s:e], woa_ref[...], preferred_element_type=jnp.float32)
         + jnp.dot(yb_e[s:e], wob_ref[...], preferred_element_type=jnp.float32)
         for s, e in zip(cuts[:-1], cuts[1:])], axis=0)
    h2 = _rms(x1, g2_ref[...])
    a_e = jnp.dot(h2.astype(jnp.bfloat16), wa_ref[...], preferred_element_type=jnp.float32)
    b = jnp.dot(h2[lo:hi].astype(jnp.bfloat16), wb_ref[...], preferred_element_type=jnp.float32)
    a_prev = pltpu.roll(a_e, 1, axis=0)[lo:hi]
    a_next = pltpu.roll(a_e, n_ext - 1, axis=0)[lo:hi]
    a_conv = (cw_ref[0:1, :] * a_prev + cw_ref[1:2, :] * a_e[lo:hi] + cw_ref[2:3, :] * a_next
              + cb_ref[...])
    gated = (_gelu(a_conv) * b).astype(jnp.bfloat16)
    for r in range(0, tm, blk):
        x2 = x1[lo + r:lo + r + blk] + jnp.dot(gated[r:r + blk], wd_ref[...],
                                               preferred_element_type=jnp.float32)
        o_ref[r:r + blk, :] = _rms(x2, gf_ref[...])


def _ffn_out(x2d, ya, yb, p, seq_len, token_offset):
    n = x2d.shape[0]
    tm = FFN_TM
    tiles_per_seq = seq_len // tm
    tile_offset = token_offset // tm

    def halo_specs(arr, rows, off):
        width = arr.shape[1]
        per = tm // rows
        last = arr.shape[0] // rows - 1
        prev = pl.BlockSpec((rows, width), lambda i: (jnp.maximum((i + off) * per - 1, 0), 0))
        cur = pl.BlockSpec((tm, width), lambda i: (i + off, 0))
        nxt = pl.BlockSpec((rows, width), lambda i: (jnp.minimum((i + off + 1) * per, last), 0))
        return [prev, cur, nxt]

    in_specs = (halo_specs(x2d, HALO_F32, 0) + halo_specs(ya, HALO_BF16, tile_offset)
                + halo_specs(yb, HALO_BF16, tile_offset)
                + [_const_spec((A_WIDTH, D_MODEL)), _const_spec((B_WIDTH, D_MODEL), (1, 0)),
                   _const_spec((1, D_MODEL)), _const_spec((D_MODEL, D_FF)),
                   _const_spec((D_MODEL, D_FF), (0, 1)), _const_spec((3, D_FF)),
                   _const_spec((1, D_FF)),
                   _const_spec((D_FF, D_MODEL)), _const_spec((1, D_MODEL))])
    return pl.pallas_call(
        functools.partial(_ffn_out_kernel, tiles_per_seq=tiles_per_seq),
        grid=(n // tm,),
        in_specs=in_specs,
        out_specs=pl.BlockSpec((tm, D_MODEL), lambda i: (i, 0)),
        out_shape=jax.ShapeDtypeStruct((n, D_MODEL), jnp.float32),
        compiler_params=pltpu.CompilerParams(dimension_semantics=("arbitrary",),
                                             vmem_limit_bytes=VMEM_LIMIT),
        name="ffn_out",
    )(x2d, x2d, x2d, ya, ya, ya, yb, yb, yb,
      p["w_out"], p["w_out"], p["g2"], p["w_up"], p["w_up"], p["conv_w"], p["conv_b"],
      p["w_down"], p["gf"])


def _prepare(norm1_g, w_in, gate_norm_g, w_spatial, b_spatial, rpb, w_out, norm2_g, w_up,
             conv_w, conv_b, w_down, final_norm_g):
    return {
        "g1": norm1_g[0][None, :],
        "w_in": w_in[0],
        "gate_g": gate_norm_g[0][None, :],
        "w_s": w_spatial[0],
        "b_s": jnp.repeat(b_spatial[0].T, HEAD_DIM, axis=1),
        "bias": _bias_tiles(rpb[0]),
        "w_out": w_out[0],
        "g2": norm2_g[0][None, :],
        "w_up": w_up[0],
        "conv_w": conv_w[0],
        "conv_b": conv_b[0][None, :],
        "w_down": w_down[0],
        "gf": final_norm_g[None, :],
    }


def kernel(x_prompt, x_sample, norm1_g, w_in, gate_norm_g, w_spatial, b_spatial, rpb, w_out,
           norm2_g, w_up, conv_w, conv_b, w_down, final_norm_g):
    p = _prepare(norm1_g, w_in, gate_norm_g, w_spatial, b_spatial, rpb, w_out, norm2_g, w_up,
                 conv_w, conv_b, w_down, final_norm_g)
    xs = [x.reshape(-1, D_MODEL) for x in (x_prompt, x_sample)]
    segs = tuple((x.shape[0], x.shape[1] // GRID_W) for x in (x_prompt, x_sample))
    (ya, q, k, v), ffn_weights = _mix_in(xs[0], xs[1], p)
    p = {**p, **ffn_weights}
    yb = _natten(q, k, v, p["bias"], segs)
    outs, offset = [], 0
    for x, x2d in zip((x_prompt, x_sample), xs):
        y = _ffn_out(x2d, ya, yb, p, x.shape[1], offset)
        outs.append(y.reshape(x.shape))
        offset += x2d.shape[0]
    return tuple(outs)
```

```python
import functools
import math

import jax
import jax.numpy as jnp
from jax import lax
from jax.experimental import pallas as pl
from jax.experimental.pallas import tpu as pltpu

D_MODEL = 1024
GRID_W = 64
A_WIDTH = 512
B_WIDTH = 512
HEAD_DIM = 64
A_GROUPS = 8
B_HEADS = 8
IN_WIDTH = 2 * A_WIDTH + 3 * B_WIDTH
CHUNK = 128
WIN_ROWS = 8
WIN_COLS = 16
D_FF = 2816
EPS = 1e-6
NEG_INF = -1e30
SQRT_HALF = math.sqrt(0.5)
LOG2E = math.log2(math.e)

LANES = 128
HEADS_PER_GROUP = 4
GROUP_W = HEADS_PER_GROUP * HEAD_DIM
N_HEAD_GROUPS = B_HEADS // HEADS_PER_GROUP
WIN_KEYS = WIN_ROWS * GRID_W

MIX_TM = 1024
MIX_ROW_BLOCKS = 4
ATT_ROWS = 32
ATT_HALO_ROWS = 8
ATT_WIN_ROWS = ATT_ROWS + 2 * ATT_HALO_ROWS
FFN_TM = 1024
FFN_ROW_BLOCKS = 4
HALO_F32 = 8
HALO_BF16 = 16
VMEM_LIMIT = 62 * 1024 * 1024


def _gelu(x):
    return 0.5 * x * (1.0 + lax.erf(x * SQRT_HALF))


def _rms(x, g):
    return x * lax.rsqrt(jnp.mean(x * x, axis=-1, keepdims=True) + EPS) * g


def _const_spec(shape, block_index=None):
    idx = (0,) * len(shape) if block_index is None else block_index
    return pl.BlockSpec(shape, lambda *_: idx, pipeline_mode=pl.Buffered(1))


def _bias_tiles_kernel(rpb_ref, o_ref):
    n_dr = 2 * WIN_ROWS - 1
    q_col = lax.broadcasted_iota(jnp.int32, (GRID_W, LANES), 0)
    lane = lax.broadcasted_iota(jnp.int32, (GRID_W, LANES), 1)
    k_col = lane % GRID_W
    win_start = jnp.clip(q_col - WIN_COLS // 2, 0, GRID_W - WIN_COLS)
    valid = (k_col >= win_start) & (k_col < win_start + WIN_COLS)
    low_half = lane < GRID_W

    def toeplitz(h, dr, lane_offset):
        if not 0 <= dr < n_dr:
            return jnp.zeros((GRID_W, LANES), jnp.float32)
        row = jnp.broadcast_to(rpb_ref[h, dr:dr + 1, :], (GRID_W, LANES))
        shift = (lane_offset - (WIN_COLS - 1)) % LANES
        return pltpu.roll(row, shift, axis=1, stride=1, stride_axis=0)

    for copy in range(2):
        for h in range(B_HEADS):
            g, hh = divmod(h, HEADS_PER_GROUP)
            for t in range(WIN_ROWS):
                dr_lo = 2 * t - copy
                tile = jnp.where(low_half, toeplitz(h, dr_lo, 0), toeplitz(h, dr_lo + 1, GRID_W))
                o_ref[copy, g, t, hh * GRID_W:(hh + 1) * GRID_W, :] = jnp.where(
                    valid, tile * LOG2E, NEG_INF)


def _bias_tiles(rpb):
    rpb_pad = jnp.pad(rpb, ((0, 0), (0, 0), (0, LANES - rpb.shape[-1])))
    shape = (2, N_HEAD_GROUPS, WIN_ROWS, HEADS_PER_GROUP * GRID_W, LANES)
    return pl.pallas_call(
        _bias_tiles_kernel,
        out_shape=jax.ShapeDtypeStruct(shape, jnp.float32),
        name="bias_tiles",
    )(rpb_pad)


def _mix_in_kernel(xa_ref, xb_ref, g1_ref, win32_ref, gg_ref, ws32_ref, bs_ref, *rest, n_tiles_a):
    n_cast = len(_FFN_WEIGHT_SLICES)
    cast_src, outs, cast_dst = rest[:n_cast], rest[n_cast:n_cast + 4], rest[n_cast + 4:2 * n_cast + 4]
    win_ref, ws_ref = rest[2 * n_cast + 4:]
    step = pl.program_id(0)

    @pl.when(step == 0)
    def _():
        win_ref[...] = win32_ref[...].astype(jnp.bfloat16)
        ws_ref[...] = ws32_ref[...].astype(jnp.bfloat16)

    tile = functools.partial(_mix_in_tile, g1_ref=g1_ref, win_ref=win_ref, gg_ref=gg_ref,
                             ws_ref=ws_ref, bs_ref=bs_ref, outs=outs, cast=(cast_src, cast_dst))
    pl.when(step < n_tiles_a)(functools.partial(tile, xa_ref))
    pl.when(step >= n_tiles_a)(functools.partial(tile, xb_ref))


def _mix_in_tile(x_ref, *, g1_ref, win_ref, gg_ref, ws_ref, bs_ref, outs, cast):
    ya_ref, q_ref, k_ref, v_ref = outs
    for src, dst in zip(*cast):
        dst[...] = src[...].astype(jnp.bfloat16)
    tm = x_ref.shape[0]
    h = _rms(x_ref[...], g1_ref[...]).astype(jnp.bfloat16)
    blk = tm // MIX_ROW_BLOCKS
    z = jnp.concatenate([jnp.dot(h[r:r + blk], win_ref[...], preferred_element_type=jnp.float32)
                         for r in range(0, tm, blk)], axis=0)
    u = _gelu(z[:, :A_WIDTH])
    va = _rms(_gelu(z[:, A_WIDTH:2 * A_WIDTH]), gg_ref[...]).astype(jnp.bfloat16)
    q_scale = HEAD_DIM ** -0.5 * LOG2E
    q_ref[...] = (z[:, 2 * A_WIDTH:2 * A_WIDTH + B_WIDTH] * q_scale).astype(jnp.bfloat16)
    k_ref[...] = z[:, 2 * A_WIDTH + B_WIDTH:2 * A_WIDTH + 2 * B_WIDTH].astype(jnp.bfloat16)
    v_ref[...] = z[:, 2 * A_WIDTH + 2 * B_WIDTH:].astype(jnp.bfloat16)

    first_half = lax.broadcasted_iota(jnp.int32, (CHUNK, LANES), 1) < HEAD_DIM
    for c in range(tm // CHUNK):
        rows = slice(c * CHUNK, (c + 1) * CHUNK)
        for j in range(A_WIDTH // LANES):
            cols = slice(j * LANES, (j + 1) * LANES)
            va_t = va[rows, cols]
            s0 = jnp.dot(ws_ref[2 * j], va_t, preferred_element_type=jnp.float32)
            s1 = jnp.dot(ws_ref[2 * j + 1], va_t, preferred_element_type=jnp.float32)
            s = jnp.where(first_half, s0, s1) + bs_ref[:, cols]
            ya_ref[rows, cols] = (u[rows, cols] * s).astype(jnp.bfloat16)


_FFN_WEIGHT_SLICES = ("w_out", "w_up", "w_down")


def _rows_per_step(rows, n_steps):
    per = HALO_BF16
    while rows % per or rows // per > n_steps:
        per += HALO_BF16
    return per


def _mix_in(xa, xb, p):
    tm = MIX_TM
    n_a, n_b = xa.shape[0] // tm, xb.shape[0] // tm
    n = xa.shape[0] + xb.shape[0]
    xa_spec = pl.BlockSpec((tm, D_MODEL), lambda i: (jnp.minimum(i, n_a - 1), 0))
    xb_spec = pl.BlockSpec((tm, D_MODEL), lambda i: (jnp.maximum(i - n_a, 0), 0))
    tok = pl.BlockSpec((tm, A_WIDTH), lambda i: (i, 0))
    out = jax.ShapeDtypeStruct((n, A_WIDTH), jnp.bfloat16)

    weights = [p[name] for name in _FFN_WEIGHT_SLICES]
    cast_specs = []
    for w in weights:
        per = _rows_per_step(w.shape[0], n_a + n_b)
        last = w.shape[0] // per - 1
        cast_specs.append(pl.BlockSpec((per, w.shape[1]),
                                       lambda i, last=last: (jnp.minimum(i, last), 0)))
    cast_shapes = [jax.ShapeDtypeStruct(w.shape, jnp.bfloat16) for w in weights]

    res = pl.pallas_call(
        functools.partial(_mix_in_kernel, n_tiles_a=n_a),
        grid=(n_a + n_b,),
        in_specs=[xa_spec, xb_spec, _const_spec((1, D_MODEL)), _const_spec((D_MODEL, IN_WIDTH)),
                  _const_spec((1, A_WIDTH)), _const_spec((A_GROUPS, CHUNK, CHUNK)),
                  _const_spec((CHUNK, A_WIDTH))] + cast_specs,
        out_specs=[tok] * 4 + cast_specs,
        out_shape=[out] * 4 + cast_shapes,
        scratch_shapes=[pltpu.VMEM((D_MODEL, IN_WIDTH), jnp.bfloat16),
                        pltpu.VMEM((A_GROUPS, CHUNK, CHUNK), jnp.bfloat16)],
        compiler_params=pltpu.CompilerParams(dimension_semantics=("arbitrary",),
                                             vmem_limit_bytes=VMEM_LIMIT),
        name="mix_in",
    )(xa, xb, p["g1"], p["w_in"], p["gate_g"], p["w_s"], p["b_s"], *weights)
    return res[:4], dict(zip(_FFN_WEIGHT_SLICES, res[4:]))


def _att_step_info(step, segs):
    (batch_a, rows_a), (_, rows_b) = segs
    per_a, per_b = rows_a // ATT_ROWS, rows_b // ATT_ROWS
    steps_a = batch_a * per_a
    in_a = step < steps_a
    step_b = jnp.maximum(step - steps_a, 0)
    i = jnp.where(in_a, step % per_a, step_b % per_b)
    n_rows = jnp.where(in_a, rows_a, rows_b)
    seq_block = jnp.where(in_a, (step // per_a) * (rows_a // ATT_HALO_ROWS),
                          batch_a * (rows_a // ATT_HALO_ROWS)
                          + (step_b // per_b) * (rows_b // ATT_HALO_ROWS))
    win_block = jnp.clip(i * (ATT_ROWS // ATT_HALO_ROWS) - 1, 0,
                         (n_rows - ATT_WIN_ROWS) // ATT_HALO_ROWS)
    return i, n_rows, win_block, seq_block


def _natten_kernel(q_ref, k_ref, v_ref, bias_ref, o_ref, *, segs):
    i, n_rows, win_block, _ = _att_step_info(pl.program_id(0), segs)
    first_row = win_block * ATT_HALO_ROWS

    row_blk = lax.broadcasted_iota(jnp.int32, (GROUP_W, GROUP_W), 0) // HEAD_DIM
    lane_blk = lax.broadcasted_iota(jnp.int32, (GROUP_W, GROUP_W), 1) // HEAD_DIM
    diag = row_blk == lane_blk
    low_half = lax.broadcasted_iota(jnp.int32, (GRID_W, LANES), 1) < HEAD_DIM

    def row_body(rl, carry):
        r = i * ATT_ROWS + rl
        rs = jnp.clip(r - WIN_ROWS // 2, 0, n_rows - WIN_ROWS)
        cls = r - rs
        bias_copy = (cls + 1) % 2
        bias_tile = (WIN_ROWS - cls) // 2
        koff = pl.multiple_of((rs - first_row) * GRID_W, GRID_W)
        qoff = pl.multiple_of(rl * GRID_W, GRID_W)
        for g in range(N_HEAD_GROUPS):
            lanes = slice(g * GROUP_W, (g + 1) * GROUP_W)
            q_g = q_ref[pl.ds(qoff, GRID_W), lanes]
            q_bd = jnp.where(diag, jnp.concatenate([q_g] * HEADS_PER_GROUP, axis=0),
                             jnp.zeros((), jnp.bfloat16))
            k_w = k_ref[pl.ds(koff, WIN_KEYS), lanes]
            s = lax.dot_general(q_bd, k_w, (((1,), (1,)), ((), ())),
                                preferred_element_type=jnp.float32)
            s = s + jnp.concatenate([bias_ref[bias_copy, g, bias_tile + jt]
                                     for jt in range(WIN_KEYS // LANES)], axis=1)
            m = jnp.max(s, axis=-1, keepdims=True)
            e = jnp.exp2(s - m)
            inv_l = 1.0 / jnp.sum(e, axis=-1, keepdims=True)
            v_w = v_ref[pl.ds(koff, WIN_KEYS), lanes]
            o = jnp.dot(e.astype(jnp.bfloat16), v_w, preferred_element_type=jnp.float32)
            tiles = []
            for t2 in range(GROUP_W // LANES):
                blk = []
                for hh in (2 * t2, 2 * t2 + 1):
                    rws = slice(hh * GRID_W, (hh + 1) * GRID_W)
                    blk.append(o[rws, t2 * LANES:(t2 + 1) * LANES] * inv_l[rws])
                tiles.append(jnp.where(low_half, blk[0], blk[1]))
            o_ref[pl.ds(qoff, GRID_W), lanes] = jnp.concatenate(tiles, axis=1).astype(jnp.bfloat16)
        return carry

    lax.fori_loop(0, ATT_ROWS, row_body, 0, unroll=True)


def _natten(q, k, v, bias, segs):
    n = q.shape[0]
    halo_tokens = ATT_HALO_ROWS * GRID_W
    q_spec = pl.BlockSpec((ATT_ROWS * GRID_W, B_WIDTH), lambda s: (s, 0))

    def kv_index(s):
        _, _, win_block, seq_block = _att_step_info(s, segs)
        return ((seq_block + win_block) * halo_tokens, 0)

    kv_spec = pl.BlockSpec((pl.Element(ATT_WIN_ROWS * GRID_W), pl.Element(B_WIDTH)), kv_index)
    return pl.pallas_call(
        functools.partial(_natten_kernel, segs=segs),
        grid=(n // (ATT_ROWS * GRID_W),),
        in_specs=[q_spec, kv_spec, kv_spec, _const_spec(bias.shape)],
        out_specs=q_spec,
        out_shape=jax.ShapeDtypeStruct((n, B_WIDTH), jnp.bfloat16),
        compiler_params=pltpu.CompilerParams(dimension_semantics=("arbitrary",),
                                             vmem_limit_bytes=VMEM_LIMIT),
        name="natten",
    )(q, k, v, bias)


def _ffn_out_kernel(xp_ref, x_ref, xn_ref, yap_ref, ya_ref, yan_ref, ybp_ref, yb_ref, ybn_ref,
                    woa_ref, wob_ref, g2_ref, wa_ref, wb_ref, cw_ref, cb_ref, wd_ref, gf_ref,
                    o_ref, *, tiles_per_seq):
    tm = x_ref.shape[0]
    j = pl.program_id(0) % tiles_per_seq
    has_prev = j > 0
    has_next = j < tiles_per_seq - 1
    lo, hi = HALO_F32, HALO_F32 + tm

    def ext(prev_ref, cur_ref, next_ref):
        prev = prev_ref[...].astype(jnp.float32)
        nxt = next_ref[...].astype(jnp.float32)
        prev = jnp.where(has_prev, prev[prev.shape[0] - HALO_F32:], 0.0)
        nxt = jnp.where(has_next, nxt[:HALO_F32], 0.0)
        return jnp.concatenate([prev, cur_ref[...].astype(jnp.float32), nxt], axis=0)

    x_e = ext(xp_ref, x_ref, xn_ref)
    ya_e = ext(yap_ref, ya_ref, yan_ref).astype(jnp.bfloat16)
    yb_e = ext(ybp_ref, yb_ref, ybn_ref).astype(jnp.bfloat16)
    n_ext = tm + 2 * HALO_F32
    blk = tm // FFN_ROW_BLOCKS
    cuts = [0] + [n_ext - blk * k for k in range(FFN_ROW_BLOCKS - 1, -1, -1)]
    x1 = jnp.concatenate(
        [x_e[s:e] + jnp.dot(ya_e[s:e], woa_ref[...], preferred_element_type=jnp.float32)
         + jnp.dot(yb_e[s:e], wob_ref[...], preferred_element_type=jnp.float32)
         for s, e in zip(cuts[:-1], cuts[1:])], axis=0)
    h2 = _rms(x1, g2_ref[...])
    a_e = jnp.dot(h2.astype(jnp.bfloat16), wa_ref[...], preferred_element_type=jnp.float32)
    b = jnp.dot(h2[lo:hi].astype(jnp.bfloat16), wb_ref[...], preferred_element_type=jnp.float32)
    a_prev = pltpu.roll(a_e, 1, axis=0)[lo:hi]
    a_next = pltpu.roll(a_e, n_ext - 1, axis=0)[lo:hi]
    a_conv = (cw_ref[0:1, :] * a_prev + cw_ref[1:2, :] * a_e[lo:hi] + cw_ref[2:3, :] * a_next
              + cb_ref[...])
    gated = (_gelu(a_conv) * b).astype(jnp.bfloat16)
    for r in range(0, tm, blk):
        x2 = x1[lo + r:lo + r + blk] + jnp.dot(gated[r:r + blk], wd_ref[...],
                                               preferred_element_type=jnp.float32)
        o_ref[r:r + blk, :] = _rms(x2, gf_ref[...])


def _ffn_out(x2d, ya, yb, p, seq_len, token_offset):
    n = x2d.shape[0]
    tm = FFN_TM
    tiles_per_seq = seq_len // tm
    tile_offset = token_offset // tm

    def halo_specs(arr, rows, off):
        width = arr.shape[1]
        per = tm // rows
        last = arr.shape[0] // rows - 1
        prev = pl.BlockSpec((rows, width), lambda i: (jnp.maximum((i + off) * per - 1, 0), 0))
        cur = pl.BlockSpec((tm, width), lambda i: (i + off, 0))
        nxt = pl.BlockSpec((rows, width), lambda i: (jnp.minimum((i + off + 1) * per, last), 0))
        return [prev, cur, nxt]

    in_specs = (halo_specs(x2d, HALO_F32, 0) + halo_specs(ya, HALO_BF16, tile_offset)
                + halo_specs(yb, HALO_BF16, tile_offset)
                + [_const_spec((A_WIDTH, D_MODEL)), _const_spec((B_WIDTH, D_MODEL), (1, 0)),
                   _const_spec((1, D_MODEL)), _const_spec((D_MODEL, D_FF)),
                   _const_spec((D_MODEL, D_FF), (0, 1)), _const_spec((3, D_FF)),
                   _const_spec((1, D_FF)),
                   _const_spec((D_FF, D_MODEL)), _const_spec((1, D_MODEL))])
    return pl.pallas_call(
        functools.partial(_ffn_out_kernel, tiles_per_seq=tiles_per_seq),
        grid=(n // tm,),
        in_specs=in_specs,
        out_specs=pl.BlockSpec((tm, D_MODEL), lambda i: (i, 0)),
        out_shape=jax.ShapeDtypeStruct((n, D_MODEL), jnp.float32),
        compiler_params=pltpu.CompilerParams(dimension_semantics=("arbitrary",),
                                             vmem_limit_bytes=VMEM_LIMIT),
        name="ffn_out",
    )(x2d, x2d, x2d, ya, ya, ya, yb, yb, yb,
      p["w_out"], p["w_out"], p["g2"], p["w_up"], p["w_up"], p["conv_w"], p["conv_b"],
      p["w_down"], p["gf"])


def _prepare(norm1_g, w_in, gate_norm_g, w_spatial, b_spatial, rpb, w_out, norm2_g, w_up,
             conv_w, conv_b, w_down, final_norm_g):
    return {
        "g1": norm1_g[0][None, :],
        "w_in": w_in[0],
        "gate_g": gate_norm_g[0][None, :],
        "w_s": w_spatial[0],
        "b_s": jnp.repeat(b_spatial[0].T, HEAD_DIM, axis=1),
        "bias": _bias_tiles(rpb[0]),
        "w_out": w_out[0],
        "g2": norm2_g[0][None, :],
        "w_up": w_up[0],
        "conv_w": conv_w[0],
        "conv_b": conv_b[0][None, :],
        "w_down": w_down[0],
        "gf": final_norm_g[None, :],
    }


def kernel(x_prompt, x_sample, norm1_g, w_in, gate_norm_g, w_spatial, b_spatial, rpb, w_out,
           norm2_g, w_up, conv_w, conv_b, w_down, final_norm_g):
    p = _prepare(norm1_g, w_in, gate_norm_g, w_spatial, b_spatial, rpb, w_out, norm2_g, w_up,
                 conv_w, conv_b, w_down, final_norm_g)
    xs = [x.reshape(-1, D_MODEL) for x in (x_prompt, x_sample)]
    segs = tuple((x.shape[0], x.shape[1] // GRID_W) for x in (x_prompt, x_sample))
    (ya, q, k, v), ffn_weights = _mix_in(xs[0], xs[1], p)
    p = {**p, **ffn_weights}
    yb = _natten(q, k, v, p["bias"], segs)
    outs, offset = [], 0
    for x, x2d in zip((x_prompt, x_sample), xs):
        y = _ffn_out(x2d, ya, yb, p, x.shape[1], offset)
        outs.append(y.reshape(x.shape))
        offset += x2d.shape[0]
    return tuple(outs)
```

```python
import functools
import math

import jax
import jax.numpy as jnp
from jax import lax
from jax.experimental import pallas as pl
from jax.experimental.pallas import tpu as pltpu

D_MODEL = 1024
GRID_W = 64
A_WIDTH = 512
B_WIDTH = 512
HEAD_DIM = 64
A_GROUPS = 8
B_HEADS = 8
IN_WIDTH = 2 * A_WIDTH + 3 * B_WIDTH
CHUNK = 128
WIN_ROWS = 8
WIN_COLS = 16
D_FF = 2816
EPS = 1e-6
NEG_INF = -1e30
SQRT_HALF = math.sqrt(0.5)
LOG2E = math.log2(math.e)

LANES = 128
HEADS_PER_GROUP = 4
GROUP_W = HEADS_PER_GROUP * HEAD_DIM
N_HEAD_GROUPS = B_HEADS // HEADS_PER_GROUP
WIN_KEYS = WIN_ROWS * GRID_W

MIX_TM = 1024
MIX_ROW_BLOCKS = 4
ATT_ROWS = 32
ATT_ROW_LAG = 12
ATT_HALO_ROWS = 8
ATT_WIN_ROWS = ATT_ROWS + 2 * ATT_HALO_ROWS
FFN_TM = 1024
FFN_ROW_BLOCKS = 4
HALO_F32 = 8
HALO_BF16 = 16
VMEM_LIMIT = 62 * 1024 * 1024


def _gelu(x):
    return 0.5 * x * (1.0 + lax.erf(x * SQRT_HALF))


def _rms(x, g):
    return x * lax.rsqrt(jnp.mean(x * x, axis=-1, keepdims=True) + EPS) * g


def _const_spec(shape, block_index=None):
    idx = (0,) * len(shape) if block_index is None else block_index
    return pl.BlockSpec(shape, lambda *_: idx, pipeline_mode=pl.Buffered(1))


def _bias_tiles_kernel(rpb_ref, o_ref):
    n_dr = 2 * WIN_ROWS - 1
    q_col = lax.broadcasted_iota(jnp.int32, (GRID_W, LANES), 0)
    lane = lax.broadcasted_iota(jnp.int32, (GRID_W, LANES), 1)
    k_col = lane % GRID_W
    win_start = jnp.clip(q_col - WIN_COLS // 2, 0, GRID_W - WIN_COLS)
    valid = (k_col >= win_start) & (k_col < win_start + WIN_COLS)
    low_half = lane < GRID_W

    def toeplitz(h, dr, lane_offset):
        if not 0 <= dr < n_dr:
            return jnp.zeros((GRID_W, LANES), jnp.float32)
        row = jnp.broadcast_to(rpb_ref[h, dr:dr + 1, :], (GRID_W, LANES))
        shift = (lane_offset - (WIN_COLS - 1)) % LANES
        return pltpu.roll(row, shift, axis=1, stride=1, stride_axis=0)

    for copy in range(2):
        for h in range(B_HEADS):
            g, hh = divmod(h, HEADS_PER_GROUP)
            for t in range(WIN_ROWS):
                dr_lo = 2 * t - copy
                tile = jnp.where(low_half, toeplitz(h, dr_lo, 0), toeplitz(h, dr_lo + 1, GRID_W))
                o_ref[copy, g, t, hh * GRID_W:(hh + 1) * GRID_W, :] = jnp.where(
                    valid, tile * LOG2E, NEG_INF)


def _bias_tiles(rpb):
    rpb_pad = jnp.pad(rpb, ((0, 0), (0, 0), (0, LANES - rpb.shape[-1])))
    shape = (2, N_HEAD_GROUPS, WIN_ROWS, HEADS_PER_GROUP * GRID_W, LANES)
    return pl.pallas_call(
        _bias_tiles_kernel,
        out_shape=jax.ShapeDtypeStruct(shape, jnp.float32),
        name="bias_tiles",
    )(rpb_pad)


def _mix_in_kernel(xa_ref, xb_ref, g1_ref, win32_ref, gg_ref, ws32_ref, bs_ref, *rest, n_tiles_a):
    n_cast = len(_FFN_WEIGHT_SLICES)
    cast_src, outs, cast_dst = rest[:n_cast], rest[n_cast:n_cast + 4], rest[n_cast + 4:2 * n_cast + 4]
    win_ref, ws_ref = rest[2 * n_cast + 4:]
    step = pl.program_id(0)

    @pl.when(step == 0)
    def _():
        win_ref[...] = win32_ref[...].astype(jnp.bfloat16)
        ws_ref[...] = ws32_ref[...].astype(jnp.bfloat16)

    tile = functools.partial(_mix_in_tile, g1_ref=g1_ref, win_ref=win_ref, gg_ref=gg_ref,
                             ws_ref=ws_ref, bs_ref=bs_ref, outs=outs, cast=(cast_src, cast_dst))
    pl.when(step < n_tiles_a)(functools.partial(tile, xa_ref))
    pl.when(step >= n_tiles_a)(functools.partial(tile, xb_ref))


def _mix_in_tile(x_ref, *, g1_ref, win_ref, gg_ref, ws_ref, bs_ref, outs, cast):
    ya_ref, q_ref, k_ref, v_ref = outs
    for src, dst in zip(*cast):
        dst[...] = src[...].astype(jnp.bfloat16)
    tm = x_ref.shape[0]
    h = _rms(x_ref[...], g1_ref[...]).astype(jnp.bfloat16)
    blk = tm // MIX_ROW_BLOCKS
    z = jnp.concatenate([jnp.dot(h[r:r + blk], win_ref[...], preferred_element_type=jnp.float32)
                         for r in range(0, tm, blk)], axis=0)
    u = _gelu(z[:, :A_WIDTH])
    va = _rms(_gelu(z[:, A_WIDTH:2 * A_WIDTH]), gg_ref[...]).astype(jnp.bfloat16)
    q_scale = HEAD_DIM ** -0.5 * LOG2E
    q_ref[...] = (z[:, 2 * A_WIDTH:2 * A_WIDTH + B_WIDTH] * q_scale).astype(jnp.bfloat16)
    k_ref[...] = z[:, 2 * A_WIDTH + B_WIDTH:2 * A_WIDTH + 2 * B_WIDTH].astype(jnp.bfloat16)
    v_ref[...] = z[:, 2 * A_WIDTH + 2 * B_WIDTH:].astype(jnp.bfloat16)

    first_half = lax.broadcasted_iota(jnp.int32, (CHUNK, LANES), 1) < HEAD_DIM
    for c in range(tm // CHUNK):
        rows = slice(c * CHUNK, (c + 1) * CHUNK)
        for j in range(A_WIDTH // LANES):
            cols = slice(j * LANES, (j + 1) * LANES)
            va_t = va[rows, cols]
            s0 = jnp.dot(ws_ref[2 * j], va_t, preferred_element_type=jnp.float32)
            s1 = jnp.dot(ws_ref[2 * j + 1], va_t, preferred_element_type=jnp.float32)
            s = jnp.where(first_half, s0, s1) + bs_ref[:, cols]
            ya_ref[rows, cols] = (u[rows, cols] * s).astype(jnp.bfloat16)


_FFN_WEIGHT_SLICES = ("w_out", "w_up", "w_down")


def _rows_per_step(rows, n_steps):
    per = HALO_BF16
    while rows % per or rows // per > n_steps:
        per += HALO_BF16
    return per


def _mix_in(xa, xb, p):
    tm = MIX_TM
    n_a, n_b = xa.shape[0] // tm, xb.shape[0] // tm
    n = xa.shape[0] + xb.shape[0]
    xa_spec = pl.BlockSpec((tm, D_MODEL), lambda i: (jnp.minimum(i, n_a - 1), 0))
    xb_spec = pl.BlockSpec((tm, D_MODEL), lambda i: (jnp.maximum(i - n_a, 0), 0))
    tok = pl.BlockSpec((tm, A_WIDTH), lambda i: (i, 0))
    out = jax.ShapeDtypeStruct((n, A_WIDTH), jnp.bfloat16)

    weights = [p[name] for name in _FFN_WEIGHT_SLICES]
    cast_specs = []
    for w in weights:
        per = _rows_per_step(w.shape[0], n_a + n_b)
        last = w.shape[0] // per - 1
        cast_specs.append(pl.BlockSpec((per, w.shape[1]),
                                       lambda i, last=last: (jnp.minimum(i, last), 0)))
    cast_shapes = [jax.ShapeDtypeStruct(w.shape, jnp.bfloat16) for w in weights]

    res = pl.pallas_call(
        functools.partial(_mix_in_kernel, n_tiles_a=n_a),
        grid=(n_a + n_b,),
        in_specs=[xa_spec, xb_spec, _const_spec((1, D_MODEL)), _const_spec((D_MODEL, IN_WIDTH)),
                  _const_spec((1, A_WIDTH)), _const_spec((A_GROUPS, CHUNK, CHUNK)),
                  _const_spec((CHUNK, A_WIDTH))] + cast_specs,
        out_specs=[tok] * 4 + cast_specs,
        out_shape=[out] * 4 + cast_shapes,
        scratch_shapes=[pltpu.VMEM((D_MODEL, IN_WIDTH), jnp.bfloat16),
                        pltpu.VMEM((A_GROUPS, CHUNK, CHUNK), jnp.bfloat16)],
        compiler_params=pltpu.CompilerParams(dimension_semantics=("arbitrary",),
                                             vmem_limit_bytes=VMEM_LIMIT),
        name="mix_in",
    )(xa, xb, p["g1"], p["w_in"], p["gate_g"], p["w_s"], p["b_s"], *weights)
    return res[:4], dict(zip(_FFN_WEIGHT_SLICES, res[4:]))


def _att_step_info(step, segs):
    (batch_a, rows_a), (_, rows_b) = segs
    per_a, per_b = rows_a // ATT_ROWS, rows_b // ATT_ROWS
    steps_a = batch_a * per_a
    in_a = step < steps_a
    step_b = jnp.maximum(step - steps_a, 0)
    i = jnp.where(in_a, step % per_a, step_b % per_b)
    n_rows = jnp.where(in_a, rows_a, rows_b)
    seq_block = jnp.where(in_a, (step // per_a) * (rows_a // ATT_HALO_ROWS),
                          batch_a * (rows_a // ATT_HALO_ROWS)
                          + (step_b // per_b) * (rows_b // ATT_HALO_ROWS))
    win_block = jnp.clip(i * (ATT_ROWS // ATT_HALO_ROWS) - 1, 0,
                         (n_rows - ATT_WIN_ROWS) // ATT_HALO_ROWS)
    return i, n_rows, win_block, seq_block


def _natten_kernel(q_ref, k_ref, v_ref, bias_ref, o_ref, *, segs):
    i, n_rows, win_block, _ = _att_step_info(pl.program_id(0), segs)
    first_row = win_block * ATT_HALO_ROWS

    row_blk = lax.broadcasted_iota(jnp.int32, (GROUP_W, GROUP_W), 0) // HEAD_DIM
    lane_blk = lax.broadcasted_iota(jnp.int32, (GROUP_W, GROUP_W), 1) // HEAD_DIM
    diag = row_blk == lane_blk
    low_half = lax.broadcasted_iota(jnp.int32, (GRID_W, LANES), 1) < HEAD_DIM

    def row_body(rl, carry):
        r = i * ATT_ROWS + rl
        rs = jnp.clip(r - WIN_ROWS // 2, 0, n_rows - WIN_ROWS)
        cls = r - rs
        bias_copy = (cls + 1) % 2
        bias_tile = (WIN_ROWS - cls) // 2
        koff = pl.multiple_of((rs - first_row) * GRID_W, GRID_W)
        qoff = rl * GRID_W
        for g in range(N_HEAD_GROUPS):
            lanes = slice(g * GROUP_W, (g + 1) * GROUP_W)
            q_g = q_ref[pl.ds(qoff, GRID_W), lanes]
            if carry is not None:
                q_g = q_g + carry
            q_bd = jnp.where(diag, jnp.concatenate([q_g] * HEADS_PER_GROUP, axis=0),
                             jnp.zeros((), jnp.bfloat16))
            k_w = k_ref[pl.ds(koff, WIN_KEYS), lanes]
            s = lax.dot_general(q_bd, k_w, (((1,), (1,)), ((), ())),
                                preferred_element_type=jnp.float32)
            s = s + jnp.concatenate([bias_ref[bias_copy, g, bias_tile + jt]
                                     for jt in range(WIN_KEYS // LANES)], axis=1)
            m = jnp.max(s, axis=-1, keepdims=True)
            e = jnp.exp2(s - m)
            inv_l = 1.0 / jnp.sum(e, axis=-1, keepdims=True)
            v_w = v_ref[pl.ds(koff, WIN_KEYS), lanes]
            o = jnp.dot(e.astype(jnp.bfloat16), v_w, preferred_element_type=jnp.float32)
            tiles = []
            for t2 in range(GROUP_W // LANES):
                blk = []
                for hh in (2 * t2, 2 * t2 + 1):
                    rws = slice(hh * GRID_W, (hh + 1) * GRID_W)
                    blk.append(o[rws, t2 * LANES:(t2 + 1) * LANES] * inv_l[rws])
                tiles.append(jnp.where(low_half, blk[0], blk[1]))
            out = jnp.concatenate(tiles, axis=1)
            o_ref[pl.ds(qoff, GRID_W), lanes] = out.astype(jnp.bfloat16)
        bits = lax.shift_right_logical(pltpu.bitcast(out, jnp.uint32), jnp.uint32(16))
        bits = lax.shift_right_logical(bits, jnp.uint32(16))
        return pltpu.bitcast(bits, jnp.float32).astype(jnp.bfloat16)

    tokens = []
    for rl in range(ATT_ROWS):
        gate = tokens[rl - ATT_ROW_LAG] if rl >= ATT_ROW_LAG else None
        tokens.append(row_body(rl, gate))


def _natten(q, k, v, bias, segs):
    n = q.shape[0]
    halo_tokens = ATT_HALO_ROWS * GRID_W
    q_spec = pl.BlockSpec((ATT_ROWS * GRID_W, B_WIDTH), lambda s: (s, 0))

    def kv_index(s):
        _, _, win_block, seq_block = _att_step_info(s, segs)
        return ((seq_block + win_block) * halo_tokens, 0)

    kv_spec = pl.BlockSpec((pl.Element(ATT_WIN_ROWS * GRID_W), pl.Element(B_WIDTH)), kv_index)
    return pl.pallas_call(
        functools.partial(_natten_kernel, segs=segs),
        grid=(n // (ATT_ROWS * GRID_W),),
        in_specs=[q_spec, kv_spec, kv_spec, _const_spec(bias.shape)],
        out_specs=q_spec,
        out_shape=jax.ShapeDtypeStruct((n, B_WIDTH), jnp.bfloat16),
        compiler_params=pltpu.CompilerParams(dimension_semantics=("arbitrary",),
                                             vmem_limit_bytes=VMEM_LIMIT),
        name="natten",
    )(q, k, v, bias)


def _ffn_out_kernel(xp_ref, x_ref, xn_ref, yap_ref, ya_ref, yan_ref, ybp_ref, yb_ref, ybn_ref,
                    woa_ref, wob_ref, g2_ref, wa_ref, wb_ref, cw_ref, cb_ref, wd_ref, gf_ref,
                    o_ref, *, tiles_per_seq):
    tm = x_ref.shape[0]
    j = pl.program_id(0) % tiles_per_seq
    has_prev = j > 0
    has_next = j < tiles_per_seq - 1
    lo, hi = HALO_F32, HALO_F32 + tm

    def ext(prev_ref, cur_ref, next_ref):
        prev = prev_ref[...].astype(jnp.float32)
        nxt = next_ref[...].astype(jnp.float32)
        prev = jnp.where(has_prev, prev[prev.shape[0] - HALO_F32:], 0.0)
        nxt = jnp.where(has_next, nxt[:HALO_F32], 0.0)
        return jnp.concatenate([prev, cur_ref[...].astype(jnp.float32), nxt], axis=0)

    x_e = ext(xp_ref, x_ref, xn_ref)
    ya_e = ext(yap_ref, ya_ref, yan_ref).astype(jnp.bfloat16)
    yb_e = ext(ybp_ref, yb_ref, ybn_ref).astype(jnp.bfloat16)
    n_ext = tm + 2 * HALO_F32
    blk = tm // FFN_ROW_BLOCKS
    cuts = [0] + [n_ext - blk * k for k in range(FFN_ROW_BLOCKS - 1, -1, -1)]
    x1 = jnp.concatenate(
        [x_e[s:e] + jnp.dot(ya_e[s:e], woa_ref[...], preferred_element_type=jnp.float32)
         + jnp.dot(yb_e[s:e], wob_ref[...], preferred_element_type=jnp.float32)
         for s, e in zip(cuts[:-1], cuts[1:])], axis=0)
    h2 = _rms(x1, g2_ref[...])
    a_e = jnp.dot(h2.astype(jnp.bfloat16), wa_ref[...], preferred_element_type=jnp.float32)
    b = jnp.dot(h2[lo:hi].astype(jnp.bfloat16), wb_ref[...], preferred_element_type=jnp.float32)
    a_prev = pltpu.roll(a_e, 1, axis=0)[lo:hi]
    a_next = pltpu.roll(a_e, n_ext - 1, axis=0)[lo:hi]
    a_conv = (cw_ref[0:1, :] * a_prev + cw_ref[1:2, :] * a_e[lo:hi] + cw_ref[2:3, :] * a_next
              + cb_ref[...])
    gated = (_gelu(a_conv) * b).astype(jnp.bfloat16)
    for r in range(0, tm, blk):
        x2 = x1[lo + r:lo + r + blk] + jnp.dot(gated[r:r + blk], wd_ref[...],
                                               preferred_element_type=jnp.float32)
        o_ref[r:r + blk, :] = _rms(x2, gf_ref[...])


def _ffn_out(x2d, ya, yb, p, seq_len, token_offset):
    n = x2d.shape[0]
    tm = FFN_TM
    tiles_per_seq = seq_len // tm
    tile_offset = token_offset // tm

    def halo_specs(arr, rows, off):
        width = arr.shape[1]
        per = tm // rows
        last = arr.shape[0] // rows - 1
        prev = pl.BlockSpec((rows, width), lambda i: (jnp.maximum((i + off) * per - 1, 0), 0))
        cur = pl.BlockSpec((tm, width), lambda i: (i + off, 0))
        nxt = pl.BlockSpec((rows, width), lambda i: (jnp.minimum((i + off + 1) * per, last), 0))
        return [prev, cur, nxt]

    in_specs = (halo_specs(x2d, HALO_F32, 0) + halo_specs(ya, HALO_BF16, tile_offset)
                + halo_specs(yb, HALO_BF16, tile_offset)
                + [_const_spec((A_WIDTH, D_MODEL)), _const_spec((B_WIDTH, D_MODEL), (1, 0)),
                   _const_spec((1, D_MODEL)), _const_spec((D_MODEL, D_FF)),
                   _const_spec((D_MODEL, D_FF), (0, 1)), _const_spec((3, D_FF)),
                   _const_spec((1, D_FF)),
                   _const_spec((D_FF, D_MODEL)), _const_spec((1, D_MODEL))])
    return pl.pallas_call(
        functools.partial(_ffn_out_kernel, tiles_per_seq=tiles_per_seq),
        grid=(n // tm,),
        in_specs=in_specs,
        out_specs=pl.BlockSpec((tm, D_MODEL), lambda i: (i, 0)),
        out_shape=jax.ShapeDtypeStruct((n, D_MODEL), jnp.float32),
        compiler_params=pltpu.CompilerParams(dimension_semantics=("arbitrary",),
                                             vmem_limit_bytes=VMEM_LIMIT),
        name="ffn_out",
    )(x2d, x2d, x2d, ya, ya, ya, yb, yb, yb,
      p["w_out"], p["w_out"], p["g2"], p["w_up"], p["w_up"], p["conv_w"], p["conv_b"],
      p["w_down"], p["gf"])


def _prepare(norm1_g, w_in, gate_norm_g, w_spatial, b_spatial, rpb, w_out, norm2_g, w_up,
             conv_w, conv_b, w_down, final_norm_g):
    return {
        "g1": norm1_g[0][None, :],
        "w_in": w_in[0],
        "gate_g": gate_norm_g[0][None, :],
        "w_s": w_spatial[0],
        "b_s": jnp.repeat(b_spatial[0].T, HEAD_DIM, axis=1),
        "bias": _bias_tiles(rpb[0]),
        "w_out": w_out[0],
        "g2": norm2_g[0][None, :],
        "w_up": w_up[0],
        "conv_w": conv_w[0],
        "conv_b": conv_b[0][None, :],
        "w_down": w_down[0],
        "gf": final_norm_g[None, :],
    }


def kernel(x_prompt, x_sample, norm1_g, w_in, gate_norm_g, w_spatial, b_spatial, rpb, w_out,
           norm2_g, w_up, conv_w, conv_b, w_down, final_norm_g):
    p = _prepare(norm1_g, w_in, gate_norm_g, w_spatial, b_spatial, rpb, w_out, norm2_g, w_up,
                 conv_w, conv_b, w_down, final_norm_g)
    xs = [x.reshape(-1, D_MODEL) for x in (x_prompt, x_sample)]
    segs = tuple((x.shape[0], x.shape[1] // GRID_W) for x in (x_prompt, x_sample))
    (ya, q, k, v), ffn_weights = _mix_in(xs[0], xs[1], p)
    p = {**p, **ffn_weights}
    yb = _natten(q, k, v, p["bias"], segs)
    outs, offset = [], 0
    for x, x2d in zip((x_prompt, x_sample), xs):
        y = _ffn_out(x2d, ya, yb, p, x.shape[1], offset)
        outs.append(y.reshape(x.shape))
        offset += x2d.shape[0]
    return tuple(outs)
```
